```python
import jax, jax.numpy as jnp
from jax import lax
import numpy as np

D_MODEL = 1024
BATCH = 16
SEQ = 4096
DEPTH = 2
DEC_BATCH = 8
DEC_SEQ = 64
PAST_LEN = 2048

CHUNK = 64
D_MIX = D_MODEL
D_CONV = D_MIX // 2
D_RWKV = D_MIX - D_CONV
HEAD_DIM = 64
N_HEADS = D_RWKV // HEAD_DIM
CONV_WIDTH = 31
CONV_BUF = min(CONV_WIDTH - 1, PAST_LEN)
LORA_W = 64
LORA_A = 64
LORA_G = 128
D_SHIFT = 3 * D_RWKV + LORA_W + LORA_A + LORA_G
D_IN = 2 * D_CONV + D_SHIFT
D_FF = -(-8 * D_MODEL // (3 * 256)) * 256
RWKV_SPLIT = (D_RWKV, 2 * D_RWKV, 3 * D_RWKV, 3 * D_RWKV + LORA_W, 3 * D_RWKV + LORA_W + LORA_A)
RMS_EPS = 1e-6
LN_EPS = 1e-5
GN_EPS = 64e-5
L2_EPS = 1e-12

kernel_name = 'hymba_conformer_rwkv7_stream_step'


def _rms_norm(x, g):
    xf = x.astype(jnp.float32)
    y = xf * lax.rsqrt(jnp.mean(xf * xf, axis=-1, keepdims=True) + RMS_EPS)
    return (y * g.astype(jnp.float32)).astype(x.dtype)


def _layer_norm(x, g, b):
    xf = x.astype(jnp.float32)
    m = jnp.mean(xf, axis=-1, keepdims=True)
    v = jnp.mean(jnp.square(xf - m), axis=-1, keepdims=True)
    y = (xf - m) * lax.rsqrt(v + LN_EPS) * g.astype(jnp.float32) + b.astype(jnp.float32)
    return y.astype(x.dtype)


def _conv_mix(u, conv_buf, conv_dw, conv_b, ln_g, ln_b):
    glu = u[..., :D_CONV] * jax.nn.sigmoid(u[..., D_CONV:])
    full = jnp.concatenate([conv_buf.astype(glu.dtype), glu], axis=1)
    y = lax.conv_general_dilated(full, conv_dw[:, None, :].astype(full.dtype), window_strides=(1,),
                                 padding='VALID', dimension_numbers=('NWC', 'WIO', 'NWC'),
                                 feature_group_count=D_CONV) + conv_b
    y = _layer_norm(y, ln_g, ln_b)
    return y * jax.nn.sigmoid(y), full[:, -CONV_BUF:]


def _wkv_step(S, inp):
    r, w, k, v, kk, kka = inp
    sa = jnp.einsum('bhvk,bhk->bhv', S, -kk)
    S = S * w[:, :, None, :] + sa[..., None] * kka[:, :, None, :] + v[..., None] * k[:, :, None, :]
    return S, jnp.einsum('bhvk,bhk->bhv', S, r)


def _rwkv_mix(z, shift_prev, wkv0, mu_shift, w0, w2, a0, a2, g2, k_k, k_a, r_k, lnx_g, lnx_b):
    B, T, _ = z.shape
    f32 = jnp.float32
    z_prev = jnp.concatenate([shift_prev[:, None, :].astype(z.dtype), z[:, :-1]], axis=1)
    zm = z + (z_prev - z) * mu_shift
    zr, zk, zv, zw, za, zg = jnp.split(zm, RWKV_SPLIT, axis=-1)
    heads = lambda t: t.astype(f32).reshape(B, T, N_HEADS, HEAD_DIM)
    w_log = -jax.nn.softplus(-(w0 + jnp.tanh(zw) @ w2).astype(f32)) - 0.5
    decay = heads(jnp.exp(-jnp.exp(w_log)))
    a = heads(jax.nn.sigmoid((a0 + za @ a2).astype(f32)))
    g = jax.nn.sigmoid(zg) @ g2
    r, k, v = heads(zr), heads(zk), heads(zv)
    kk = k * k_k.astype(f32).reshape(N_HEADS, HEAD_DIM)
    kk = kk * lax.rsqrt(jnp.sum(kk * kk, axis=-1, keepdims=True) + L2_EPS)
    k = k * (1.0 + (a - 1.0) * k_a.astype(f32).reshape(N_HEADS, HEAD_DIM))
    tm = lambda t: jnp.moveaxis(t, 1, 0)
    S_fin, ys = lax.scan(_wkv_step, wkv0.astype(f32), (tm(r), tm(decay), tm(k), tm(v), tm(kk), tm(kk * a)))
    y = jnp.moveaxis(ys, 0, 1)
    m = jnp.mean(y, axis=-1, keepdims=True)
    var = jnp.mean(jnp.square(y - m), axis=-1, keepdims=True)
    y = ((y - m) * lax.rsqrt(var + GN_EPS)).reshape(B, T, D_RWKV) * lnx_g + lnx_b
    bonus = (jnp.sum(r * k * r_k.astype(f32), axis=-1, keepdims=True) * v).reshape(B, T, D_RWKV)
    out = ((y + bonus) * g.astype(f32)).astype(z.dtype)
    return out, z[:, -1], S_fin.astype(wkv0.dtype)


def _layer(x, c, conv_buf, shift_prev, wkv, w_mod, b_mod, g_mix_pre, g_mix_post, g_ffn_pre, g_ffn_post,
           w_in, conv_dw, conv_b, conv_ln_g, conv_ln_b, mu_shift, w0, w2, a0, a2, g2, k_k, k_a, r_k,
           lnx_g, lnx_b, w_out, w_gate, w_up, w_down):
    mod = jax.nn.silu(c) @ w_mod + b_mod
    sh1, sc1, ga1, sh2, sc2, ga2 = [m[:, None, :] for m in jnp.split(mod, 6, axis=-1)]
    h = _rms_norm(x, g_mix_pre) * (1.0 + sc1) + sh1
    u = h @ w_in
    conv_out, new_buf = _conv_mix(u[..., :2 * D_CONV], conv_buf, conv_dw, conv_b, conv_ln_g, conv_ln_b)
    rwkv_out, new_shift, new_wkv = _rwkv_mix(u[..., 2 * D_CONV:], shift_prev, wkv, mu_shift, w0, w2,
                                             a0, a2, g2, k_k, k_a, r_k, lnx_g, lnx_b)
    mix = jnp.concatenate([conv_out, rwkv_out], axis=-1) @ w_out
    x = x + (1.0 + ga1) * _rms_norm(mix, g_mix_post)
    h = _rms_norm(x, g_ffn_pre) * (1.0 + sc2) + sh2
    f = (jax.nn.silu(h @ w_gate) * (h @ w_up)) @ w_down
    x = x + (1.0 + ga2) * _rms_norm(f, g_ffn_post)
    return x, new_buf, new_shift, new_wkv


def setup_inputs(seed: int = 0) -> dict:
    key = jax.random.key(seed)
    ks = iter(jax.random.split(key, 48))
    f32 = jnp.float32
    nrm = lambda shape, s: jax.random.normal(next(ks), shape, f32) * s
    L = DEPTH
    return {
        'x_prompt': nrm((BATCH, SEQ, D_MODEL), 1.0),
        'x_sample': nrm((DEC_BATCH, DEC_SEQ, D_MODEL), 1.0),
        'cache_conv': nrm((L, DEC_BATCH, CONV_BUF, D_CONV), 0.5),
        'state_shift': nrm((L, DEC_BATCH, D_SHIFT), 1.0),
        'state_wkv': nrm((L, DEC_BATCH, N_HEADS, HEAD_DIM, HEAD_DIM), 0.3),
        'c_prompt': nrm((BATCH, D_MODEL), 1.0),
        'c_sample': nrm((DEC_BATCH, D_MODEL), 1.0),
        'w_mod': nrm((L, D_MODEL, 6 * D_MODEL), 0.1 * D_MODEL ** -0.5),
        'b_mod': nrm((L, 6 * D_MODEL), 0.01),
        'g_mix_pre': 1.0 + nrm((L, D_MODEL), 0.05),
        'g_mix_post': 1.0 + nrm((L, D_MODEL), 0.05),
        'g_ffn_pre': 1.0 + nrm((L, D_MODEL), 0.05),
        'g_ffn_post': 1.0 + nrm((L, D_MODEL), 0.05),
        'w_in': nrm((L, D_MODEL, D_IN), D_MODEL ** -0.5),
        'conv_dw': nrm((L, CONV_WIDTH, D_CONV), CONV_WIDTH ** -0.5),
        'conv_b': nrm((L, D_CONV), 0.01),
        'conv_ln_g': 1.0 + nrm((L, D_CONV), 0.05),
        'conv_ln_b': nrm((L, D_CONV), 0.01),
        'mu_shift': jax.random.uniform(next(ks), (L, D_SHIFT), f32),
        'w0': jax.random.uniform(next(ks), (L, D_RWKV), f32, -5.0, 1.0),
        'w2': nrm((L, LORA_W, D_RWKV), 0.5 * LORA_W ** -0.5),
        'a0': nrm((L, D_RWKV), 0.3),
        'a2': nrm((L, LORA_A, D_RWKV), 0.5 * LORA_A ** -0.5),
        'g2': nrm((L, LORA_G, D_RWKV), LORA_G ** -0.5),
        'k_k': 0.85 + nrm((L, D_RWKV), 0.05),
        'k_a': 1.0 + nrm((L, D_RWKV), 0.05),
        'r_k': nrm((L, N_HEADS, HEAD_DIM), 0.1),
        'lnx_g': 1.0 + nrm((L, D_RWKV), 0.05),
        'lnx_b': nrm((L, D_RWKV), 0.01),
        'w_out': nrm((L, D_MIX, D_MODEL), D_MIX ** -0.5),
        'w_gate': nrm((L, D_MODEL, D_FF), D_MODEL ** -0.5),
        'w_up': nrm((L, D_MODEL, D_FF), D_MODEL ** -0.5),
        'w_down': nrm((L, D_FF, D_MODEL), D_FF ** -0.5),
    }


def reference(x_prompt, x_sample, cache_conv, state_shift, state_wkv, c_prompt, c_sample,
              w_mod, b_mod, g_mix_pre, g_mix_post, g_ffn_pre, g_ffn_post, w_in, conv_dw, conv_b,
              conv_ln_g, conv_ln_b, mu_shift, w0, w2, a0, a2, g2, k_k, k_a, r_k, lnx_g, lnx_b,
              w_out, w_gate, w_up, w_down):
    weights = (w_mod, b_mod, g_mix_pre, g_mix_post, g_ffn_pre, g_ffn_post, w_in, conv_dw, conv_b,
               conv_ln_g, conv_ln_b, mu_shift, w0, w2, a0, a2, g2, k_k, k_a, r_k, lnx_g, lnx_b,
               w_out, w_gate, w_up, w_down)
    bp = x_prompt.shape[0]
    dt = x_prompt.dtype
    zero_conv = jnp.zeros((bp, CONV_BUF, D_CONV), dt)
    zero_shift = jnp.zeros((bp, D_SHIFT), dt)
    zero_wkv = jnp.zeros((bp, N_HEADS, HEAD_DIM, HEAD_DIM), dt)
    yp, ys = x_prompt, x_sample
    cp, sp, wp, cs, ss, wsm = [], [], [], [], [], []
    for l in range(DEPTH):
        lw = [w[l] for w in weights]
        yp, b1, s1, k1 = _layer(yp, c_prompt, zero_conv, zero_shift, zero_wkv, *lw)
        ys, b2, s2, k2 = _layer(ys, c_sample, cache_conv[l], state_shift[l], state_wkv[l], *lw)
        cp.append(b1); sp.append(s1); wp.append(k1)
        cs.append(b2); ss.append(s2); wsm.append(k2)
    conv_prompt, shift_prompt, wkv_prompt = jnp.stack(cp), jnp.stack(sp), jnp.stack(wp)
    conv_sample, shift_sample, wkv_sample = jnp.stack(cs), jnp.stack(ss), jnp.stack(wsm)
    return (yp, ys, conv_prompt, shift_prompt, wkv_prompt, conv_sample, shift_sample, wkv_sample)
```

```python
import functools
import math

import jax
import jax.numpy as jnp
from jax import lax
from jax.experimental import pallas as pl
from jax.experimental.pallas import tpu as pltpu

F32 = jnp.float32
BF16 = jnp.bfloat16

D_MODEL = 1024
D_CONV = 512
D_RWKV = 512
HEAD_DIM = 64
N_HEADS = 8
CONV_WIDTH = 31
CONV_BUF = 30
LORA_W = 64
LORA_A = 64
LORA_G = 128
D_SHIFT = 3 * D_RWKV + LORA_W + LORA_A + LORA_G
D_IN = 2 * D_CONV + D_SHIFT
RMS_EPS = 1e-6
LN_EPS = 1e-5
GN_EPS = 64e-5
L2_EPS = 1e-12

CHUNK = 64
QUAD = 4 * HEAD_DIM
CONV_HALO = 32
VMEM_LIMIT = 56 * 1024 * 1024


def _dot(a, b):
    return jnp.dot(a, b, preferred_element_type=F32)


def _dot_nt(a, b):
    return lax.dot_general(a, b, (((1,), (1,)), ((), ())), preferred_element_type=F32)


def _dot_tn(a, b):
    return lax.dot_general(a, b, (((0,), (0,)), ((), ())), preferred_element_type=F32)


def _split3(x):
    hi = x.astype(BF16)
    r1 = x - hi.astype(F32)
    mid = r1.astype(BF16)
    lo = (r1 - mid.astype(F32)).astype(BF16)
    return hi, mid, lo


def _dot_exact_rhs(x, m):
    hi, mid, lo = _split3(x)
    return _dot(hi, m) + _dot(mid, m) + _dot(lo, m)


def _dot_exact_lhs(m, x):
    hi, mid, lo = _split3(x)
    return _dot(m, hi) + _dot(m, mid) + _dot(m, lo)


def _sigmoid(x):
    return 1.0 / (1.0 + jnp.exp(-x))


def _rms(x, g):
    return x * lax.rsqrt(jnp.mean(x * x, axis=-1, keepdims=True) + RMS_EPS) * g


def _mod_kernel(c_ref, w_ref, b_ref, o_ref):
    c = c_ref[...]
    s = c * _sigmoid(c)
    o_ref[0] = jnp.dot(s, w_ref[0], preferred_element_type=F32,
                       precision=lax.Precision.HIGHEST) + b_ref[0]


def _modulation(c_all, w_mod, b_mod):
    L = w_mod.shape[0]
    bc = c_all.shape[0]
    n = w_mod.shape[2]
    tn = 1024
    return pl.pallas_call(
        _mod_kernel,
        grid=(L, n // tn),
        in_specs=[
            pl.BlockSpec((bc, D_MODEL), lambda l, j: (0, 0)),
            pl.BlockSpec((1, D_MODEL, tn), lambda l, j: (l, 0, j)),
            pl.BlockSpec((1, 1, tn), lambda l, j: (l, 0, j)),
        ],
        out_specs=pl.BlockSpec((1, bc, tn), lambda l, j: (l, 0, j)),
        out_shape=jax.ShapeDtypeStruct((L, bc, n), F32),
        compiler_params=pltpu.CompilerParams(
            dimension_semantics=("arbitrary", "arbitrary"), vmem_limit_bytes=VMEM_LIMIT),
        name="modulation",
    )(c_all, w_mod, b_mod.reshape(L, 1, n))


def _inproj_kernel(x_ref, mod_ref, g_ref, w_ref, glu_ref, z_ref):
    x = x_ref[0]
    m = mod_ref[0]
    h = _rms(x, g_ref[...]) * (1.0 + m[1:2]) + m[0:1]
    hb = h.astype(BF16)
    u = _dot(hb, w_ref[:, :2 * D_CONV])
    glu_ref[0] = u[:, :D_CONV] * _sigmoid(u[:, D_CONV:])
    z_ref[0] = _dot(hb, w_ref[:, 2 * D_CONV:])


def _inproj(x, mod, g, w_in_bf, tm):
    B, T, _ = x.shape
    return pl.pallas_call(
        _inproj_kernel,
        grid=(B, T // tm),
        in_specs=[
            pl.BlockSpec((1, tm, D_MODEL), lambda b, t: (b, t, 0)),
            pl.BlockSpec((1, 6, D_MODEL), lambda b, t: (b, 0, 0)),
            pl.BlockSpec((1, D_MODEL), lambda b, t: (0, 0)),
            pl.BlockSpec((D_MODEL, D_IN), lambda b, t: (0, 0)),
        ],
        out_specs=[
            pl.BlockSpec((1, tm, D_CONV), lambda b, t: (b, t, 0)),
            pl.BlockSpec((1, tm, D_SHIFT), lambda b, t: (b, t, 0)),
        ],
        out_shape=[
            jax.ShapeDtypeStruct((B, T, D_CONV), F32),
            jax.ShapeDtypeStruct((B, T, D_SHIFT), F32),
        ],
        compiler_params=pltpu.CompilerParams(
            dimension_semantics=("arbitrary", "arbitrary"), vmem_limit_bytes=VMEM_LIMIT),
        name="inproj",
    )(x, mod, g, w_in_bf)


def _conv_kernel(glu_ref, buf_ref, dw_ref, cb_ref, lg_ref, lb_ref, out_ref, nbuf_ref, win_ref,
                 *, tt, rb):
    t = pl.program_id(1)
    pad = CONV_HALO - CONV_BUF

    @pl.when(t == 0)
    def _():
        win_ref[0:pad, :] = jnp.zeros((pad, D_CONV), F32)
        win_ref[pad:CONV_HALO, :] = buf_ref[0]

    win_ref[CONV_HALO:CONV_HALO + tt, :] = glu_ref[0]

    for r in range(tt // rb):
        acc = jnp.broadcast_to(cb_ref[...], (rb, D_CONV))
        for j in range(CONV_WIDTH):
            lo = r * rb + pad + j
            acc = acc + win_ref[lo:lo + rb, :] * dw_ref[j:j + 1, :]
        mu = jnp.mean(acc, axis=-1, keepdims=True)
        d = acc - mu
        var = jnp.mean(d * d, axis=-1, keepdims=True)
        y = d * lax.rsqrt(var + LN_EPS) * lg_ref[...] + lb_ref[...]
        out_ref[0, r * rb:(r + 1) * rb, :] = y * _sigmoid(y)

    win_ref[0:CONV_HALO, :] = win_ref[tt:tt + CONV_HALO, :]

    @pl.when(t == pl.num_programs(1) - 1)
    def _():
        nbuf_ref[0] = win_ref[pad:CONV_HALO, :]


def _conv_mix(glu, buf, dw, cb, lg, lb, tt):
    B, T, _ = glu.shape
    rb = min(tt, 64)
    row = lambda b, t: (0, 0)
    return pl.pallas_call(
        functools.partial(_conv_kernel, tt=tt, rb=rb),
        grid=(B, T // tt),
        in_specs=[
            pl.BlockSpec((1, tt, D_CONV), lambda b, t: (b, t, 0)),
            pl.BlockSpec((1, CONV_BUF, D_CONV), lambda b, t: (b, 0, 0)),
            pl.BlockSpec((CONV_WIDTH, D_CONV), row),
            pl.BlockSpec((1, D_CONV), row),
            pl.BlockSpec((1, D_CONV), row),
            pl.BlockSpec((1, D_CONV), row),
        ],
        out_specs=[
            pl.BlockSpec((1, tt, D_CONV), lambda b, t: (b, t, 0)),
            pl.BlockSpec((1, CONV_BUF, D_CONV), lambda b, t: (b, 0, 0)),
        ],
        out_shape=[
            jax.ShapeDtypeStruct((B, T, D_CONV), F32),
            jax.ShapeDtypeStruct((B, CONV_BUF, D_CONV), F32),
        ],
        scratch_shapes=[pltpu.VMEM((CONV_HALO + tt, D_CONV), F32)],
        compiler_params=pltpu.CompilerParams(
            dimension_semantics=("arbitrary", "arbitrary"), vmem_limit_bytes=VMEM_LIMIT),
        name="conv_mix",
    )(glu, buf, dw, cb, lg, lb)


def _block_diag(x, bdmask):
    return jnp.where(bdmask, jnp.concatenate([x, x, x, x], axis=0), jnp.zeros((), x.dtype))


def _rwkv_kernel(z_ref, sprev_ref, s0_ref, mu_ref, w0_ref, wa_ref, a0_ref, g2_ref, kk_ref, ka_ref,
                 rk_ref, lg_ref, lb_ref, ones_ref, tri_ref,
                 out_ref, nshift_ref, sout_ref,
                 carry_ref, bds_ref, ra_ref, aa_ref, bb_ref, kt_ref, bh_ref, kh_ref, v_ref,
                 pc_ref, y_ref, *, tt):
    t = pl.program_id(1)
    nq = D_RWKV // QUAD
    rowq = lax.broadcasted_iota(jnp.int32, (QUAD, QUAD), 0)
    colq = lax.broadcasted_iota(jnp.int32, (QUAD, QUAD), 1)
    bdmask = (rowq // HEAD_DIM) == (colq // HEAD_DIM)
    eye = rowq == colq

    @pl.when(t == 0)
    def _():
        carry_ref[...] = sprev_ref[0]
        for q in range(nq):
            s0 = s0_ref[0, q]
            bds_ref[q] = jnp.where(bdmask, jnp.concatenate([s0, s0, s0, s0], axis=0), 0.0)

    z = z_ref[0]
    rows = lax.broadcasted_iota(jnp.int32, (tt, 1), 0)
    z_prev = jnp.where(rows == 0, carry_ref[...], pltpu.roll(z, 1, 0))
    carry_ref[...] = z[tt - 1:tt, :]
    nshift_ref[0] = z[tt - 1:tt, :]
    zm = z + (z_prev - z) * mu_ref[...]
    r = zm[:, 0:D_RWKV]
    k = zm[:, D_RWKV:2 * D_RWKV]
    v = zm[:, 2 * D_RWKV:3 * D_RWKV]
    zwa = zm[:, 3 * D_RWKV:3 * D_RWKV + LORA_W + LORA_A]
    zg = zm[:, 3 * D_RWKV + LORA_W + LORA_A:]
    lane = lax.broadcasted_iota(jnp.int32, zwa.shape, 1)
    zwa = jnp.where(lane < LORA_W, jnp.tanh(zwa), zwa)
    wa = _dot(zwa.astype(BF16), wa_ref[...])
    lw = -math.exp(-0.5) * _sigmoid(w0_ref[...] + wa[:, :D_RWKV])
    a = _sigmoid(a0_ref[...] + wa[:, D_RWKV:])
    g = _dot(_sigmoid(zg).astype(BF16), g2_ref[...])

    ones = ones_ref[...]
    kk = k * kk_ref[...]
    kk = kk * lax.rsqrt(_dot_exact_rhs(kk * kk, ones) + L2_EPS)
    k = k * (1.0 + (a - 1.0) * ka_ref[...])
    b = kk * a
    bonus = _dot_exact_rhs(r * k * rk_ref[...], ones) * v

    cum = _dot_exact_lhs(tri_ref[...], lw)
    nchunk = tt // CHUNK
    cend = cum.reshape(nchunk, CHUNK, D_RWKV)[:, CHUNK - 1:CHUNK, :]
    cend = jnp.broadcast_to(cend, (nchunk, CHUNK, D_RWKV)).reshape(tt, D_RWKV)
    e_neg = jnp.exp(-cum)
    e_end = jnp.exp(cend - cum)
    ra_ref[...] = (r * jnp.exp(cum)).astype(BF16)
    aa_ref[...] = (-kk * jnp.exp(cum - lw)).astype(BF16)
    bb_ref[...] = (b * e_neg).astype(BF16)
    kt_ref[...] = (k * e_neg).astype(BF16)
    bh_ref[...] = (b * e_end).astype(BF16)
    kh_ref[...] = (k * e_end).astype(BF16)
    v_ref[...] = v.astype(BF16)
    pc_ref[...] = jnp.exp(cend)

    rowc = lax.broadcasted_iota(jnp.int32, (CHUNK, QUAD), 0)
    colc = lax.broadcasted_iota(jnp.int32, (CHUNK, QUAD), 1) % CHUNK
    strict = colc < rowc
    incl = colc <= rowc
    ident = jnp.where(colc == rowc, 1.0, 0.0)

    def bd(x):
        return _block_diag(x.astype(BF16), bdmask)

    def mm(x, ybd):
        return _dot(x.astype(BF16), ybd)

    def chunk_body(c, carry):
        r0 = pl.multiple_of(c * CHUNK, CHUNK)
        rs = pl.ds(r0, CHUNK)
        for q in range(nq):
            cs = slice(q * QUAD, (q + 1) * QUAD)
            ra, aa, bb, kt = ra_ref[rs, cs], aa_ref[rs, cs], bb_ref[rs, cs], kt_ref[rs, cs]
            bh, kh, vv = bh_ref[rs, cs], kh_ref[rs, cs], v_ref[rs, cs]
            lhs = jnp.concatenate([aa, ra], axis=0)
            sb = _dot_nt(lhs, _block_diag(bb, bdmask))
            sk = _dot_nt(lhs, _block_diag(kt, bdmask))
            a_ab = jnp.where(strict, sb[:CHUNK], 0.0)
            a_rb = jnp.where(incl, sb[CHUNK:], 0.0)
            a_ak = jnp.where(strict, sk[:CHUNK], 0.0)
            a_rk = jnp.where(incl, sk[CHUNK:], 0.0)
            tm = ident + a_ab
            npow = mm(a_ab, bd(a_ab))
            for _ in range(4):
                res = mm(jnp.concatenate([tm, npow], axis=0), bd(npow))
                tm = tm + res[:CHUNK]
                npow = res[CHUNK:]
            tm = tm + mm(tm, bd(npow))
            vbd = _block_diag(vv, bdmask)
            av = mm(a_ak, vbd)
            w1 = mm(tm, bd(aa))
            w2 = mm(tm, bd(av))
            w1b = w1.astype(BF16)
            w2b = w2.astype(BF16)
            qm = ra.astype(F32) + mm(a_rb, _block_diag(w1b, bdmask))
            yi = mm(a_rb, _block_diag(w2b, bdmask)) + mm(a_rk, vbd)
            pc = pc_ref[pl.ds(r0, 1), cs]
            bdm = jnp.where(eye, pc, 0.0) + jnp.where(bdmask, _dot_tn(bh, w1b), 0.0)
            bdn = jnp.where(bdmask, _dot_tn(bh, w2b) + _dot_tn(kh, vv), 0.0)
            s_old = bds_ref[q].astype(BF16)
            y_ref[rs, cs] = yi + mm(qm, s_old)
            bds_ref[q] = mm(bdm, s_old) + bdn
        return carry

    lax.fori_loop(0, nchunk, chunk_body, 0)

    y = y_ref[...]
    mean = _dot_exact_rhs(y, ones) * (1.0 / HEAD_DIM)
    d = y - mean
    var = _dot_exact_rhs(d * d, ones) * (1.0 / HEAD_DIM)
    yn = d * lax.rsqrt(var + GN_EPS) * lg_ref[...] + lb_ref[...]
    out_ref[0] = (yn + bonus) * g

    @pl.when(t == pl.num_programs(1) - 1)
    def _():
        for q in range(nq):
            s = bds_ref[q]
            sout_ref[0, q] = (s[0:64] + s[64:128]) + (s[128:192] + s[192:256])


def _rwkv_mix(z, shift_prev, s0q, mu, w0, wa_bf, a0, g2_bf, k_k, k_a, r_k, lnx_g, lnx_b, tt):
    B, T, _ = z.shape
    nq = D_RWKV // QUAD
    idx = jnp.arange(D_RWKV) // HEAD_DIM
    ones = (idx[:, None] == idx[None, :]).astype(BF16)
    ti = jnp.arange(tt)
    tri = ((ti[:, None] // CHUNK == ti[None, :] // CHUNK) & (ti[None, :] <= ti[:, None])).astype(BF16)
    row = lambda b, t: (0, 0)
    vec = pl.BlockSpec((1, D_RWKV), row)
    return pl.pallas_call(
        functools.partial(_rwkv_kernel, tt=tt),
        grid=(B, T // tt),
        in_specs=[
            pl.BlockSpec((1, tt, D_SHIFT), lambda b, t: (b, t, 0)),
            pl.BlockSpec((1, 1, D_SHIFT), lambda b, t: (b, 0, 0)),
            pl.BlockSpec((1, nq, HEAD_DIM, QUAD), lambda b, t: (b, 0, 0, 0)),
            pl.BlockSpec((1, D_SHIFT), row),
            vec,
            pl.BlockSpec((LORA_W + LORA_A, 2 * D_RWKV), row),
            vec,
            pl.BlockSpec((LORA_G, D_RWKV), row),
            vec, vec, vec, vec, vec,
            pl.BlockSpec((D_RWKV, D_RWKV), row),
            pl.BlockSpec((tt, tt), row),
        ],
        out_specs=[
            pl.BlockSpec((1, tt, D_RWKV), lambda b, t: (b, t, 0)),
            pl.BlockSpec((1, 1, D_SHIFT), lambda b, t: (b, 0, 0)),
            pl.BlockSpec((1, nq, HEAD_DIM, QUAD), lambda b, t: (b, 0, 0, 0)),
        ],
        out_shape=[
            jax.ShapeDtypeStruct((B, T, D_RWKV), F32),
            jax.ShapeDtypeStruct((B, 1, D_SHIFT), F32),
            jax.ShapeDtypeStruct((B, nq, HEAD_DIM, QUAD), F32),
        ],
        scratch_shapes=[
            pltpu.VMEM((1, D_SHIFT), F32),
            pltpu.VMEM((nq, QUAD, QUAD), F32),
            pltpu.VMEM((tt, D_RWKV), BF16),
            pltpu.VMEM((tt, D_RWKV), BF16),
            pltpu.VMEM((tt, D_RWKV), BF16),
            pltpu.VMEM((tt, D_RWKV), BF16),
            pltpu.VMEM((tt, D_RWKV), BF16),
            pltpu.VMEM((tt, D_RWKV), BF16),
            pltpu.VMEM((tt, D_RWKV), BF16),
            pltpu.VMEM((tt, D_RWKV), F32),
            pltpu.VMEM((tt, D_RWKV), F32),
        ],
        compiler_params=pltpu.CompilerParams(
            dimension_semantics=("arbitrary", "arbitrary"), vmem_limit_bytes=VMEM_LIMIT),
        name="rwkv_mix",
    )(z, shift_prev, s0q, mu, w0, wa_bf, a0, g2_bf, k_k, k_a, r_k, lnx_g, lnx_b, ones, tri)


def _out_ffn_kernel(x_ref, co_ref, ro_ref, mod_ref, gpost_ref, gfpre_ref, gfpost_ref,
                    wo_ref, wg_ref, wu_ref, wd_ref, y_ref):
    x = x_ref[0]
    m = mod_ref[0]
    mix = _dot(co_ref[0].astype(BF16), wo_ref[:D_CONV, :]) + _dot(ro_ref[0].astype(BF16), wo_ref[D_CONV:, :])
    x = x + (1.0 + m[2:3]) * _rms(mix, gpost_ref[...])
    h = (_rms(x, gfpre_ref[...]) * (1.0 + m[4:5]) + m[3:4]).astype(BF16)
    gate = _dot(h, wg_ref[...])
    up = _dot(h, wu_ref[...])
    act = (gate * _sigmoid(gate) * up).astype(BF16)
    f = _dot(act, wd_ref[...])
    y_ref[0] = x + (1.0 + m[5:6]) * _rms(f, gfpost_ref[...])


def _out_ffn(x, conv_out, rwkv_out, mod, g_post, g_fpre, g_fpost, wo, wg, wu, wd, tm):
    B, T, _ = x.shape
    d_ff = wg.shape[1]
    row = lambda b, t: (0, 0)
    once = dict(pipeline_mode=pl.Buffered(1))
    return pl.pallas_call(
        _out_ffn_kernel,
        grid=(B, T // tm),
        in_specs=[
            pl.BlockSpec((1, tm, D_MODEL), lambda b, t: (b, t, 0)),
            pl.BlockSpec((1, tm, D_CONV), lambda b, t: (b, t, 0)),
            pl.BlockSpec((1, tm, D_RWKV), lambda b, t: (b, t, 0)),
            pl.BlockSpec((1, 6, D_MODEL), lambda b, t: (b, 0, 0)),
            pl.BlockSpec((1, D_MODEL), row),
            pl.BlockSpec((1, D_MODEL), row),
            pl.BlockSpec((1, D_MODEL), row),
            pl.BlockSpec((D_MODEL, D_MODEL), row, **once),
            pl.BlockSpec((D_MODEL, d_ff), row, **once),
            pl.BlockSpec((D_MODEL, d_ff), row, **once),
            pl.BlockSpec((d_ff, D_MODEL), row, **once),
        ],
        out_specs=pl.BlockSpec((1, tm, D_MODEL), lambda b, t: (b, t, 0)),
        out_shape=jax.ShapeDtypeStruct((B, T, D_MODEL), F32),
        compiler_params=pltpu.CompilerParams(
            dimension_semantics=("arbitrary", "arbitrary"), vmem_limit_bytes=VMEM_LIMIT),
        name="out_ffn",
    )(x, conv_out, rwkv_out, mod, g_post, g_fpre, g_fpost, wo, wg, wu, wd)


def _state_to_quads(s):
    B = s.shape[0]
    s = s.reshape(B, N_HEADS // 4, 4, HEAD_DIM, HEAD_DIM)
    return s.transpose(0, 1, 4, 2, 3).reshape(B, N_HEADS // 4, HEAD_DIM, QUAD)


def _quads_to_state(sq):
    B = sq.shape[0]
    s = sq.reshape(B, N_HEADS // 4, HEAD_DIM, 4, HEAD_DIM)
    return s.transpose(0, 1, 3, 4, 2).reshape(B, N_HEADS, HEAD_DIM, HEAD_DIM)


def _layer(x, mod, conv_buf, shift_prev, wkv, p, tiles):
    tm, tc, tr = tiles
    glu, z = _inproj(x, mod, p["g_mix_pre"], p["w_in"], tm)
    conv_out, new_buf = _conv_mix(glu, conv_buf, p["conv_dw"], p["conv_b"], p["conv_ln_g"],
                                  p["conv_ln_b"], tc)
    rwkv_out, new_shift, sq = _rwkv_mix(
        z, shift_prev[:, None, :], _state_to_quads(wkv), p["mu_shift"], p["w0"], p["wa"], p["a0"],
        p["g2"], p["k_k"], p["k_a"], p["r_k"], p["lnx_g"], p["lnx_b"], tr)
    y = _out_ffn(x, conv_out, rwkv_out, mod, p["g_mix_post"], p["g_ffn_pre"], p["g_ffn_post"],
                 p["w_out"], p["w_gate"], p["w_up"], p["w_down"], tm)
    return y, new_buf, new_shift[:, 0, :], _quads_to_state(sq)


def _layer_params(l, w):
    row = lambda a: a[l].reshape(1, -1)
    wa = jnp.zeros((LORA_W + LORA_A, 2 * D_RWKV), F32)
    wa = wa.at[:LORA_W, :D_RWKV].set(w["w2"][l]).at[LORA_W:, D_RWKV:].set(w["a2"][l])
    return {
        "g_mix_pre": row(w["g_mix_pre"]), "g_mix_post": row(w["g_mix_post"]),
        "g_ffn_pre": row(w["g_ffn_pre"]), "g_ffn_post": row(w["g_ffn_post"]),
        "w_in": w["w_in"][l].astype(BF16),
        "conv_dw": w["conv_dw"][l], "conv_b": row(w["conv_b"]),
        "conv_ln_g": row(w["conv_ln_g"]), "conv_ln_b": row(w["conv_ln_b"]),
        "mu_shift": row(w["mu_shift"]), "w0": row(w["w0"]), "wa": wa.astype(BF16),
        "a0": row(w["a0"]), "g2": w["g2"][l].astype(BF16),
        "k_k": row(w["k_k"]), "k_a": row(w["k_a"]), "r_k": row(w["r_k"]),
        "lnx_g": row(w["lnx_g"]), "lnx_b": row(w["lnx_b"]),
        "w_out": w["w_out"][l].astype(BF16), "w_gate": w["w_gate"][l].astype(BF16),
        "w_up": w["w_up"][l].astype(BF16), "w_down": w["w_down"][l].astype(BF16),
    }


def _tiles(T):
    tm = min(T, 512)
    tc = min(T, 256)
    tr = min(T, 256)
    return tm, tc, tr


def kernel(x_prompt, x_sample, cache_conv, state_shift, state_wkv, c_prompt, c_sample, w_mod, b_mod, g_mix_pre, g_mix_post, g_ffn_pre, g_ffn_post, w_in, conv_dw, conv_b, conv_ln_g, conv_ln_b, mu_shift, w0, w2, a0, a2, g2, k_k, k_a, r_k, lnx_g, lnx_b, w_out, w_gate, w_up, w_down):
    w = dict(g_mix_pre=g_mix_pre, g_mix_post=g_mix_post, g_ffn_pre=g_ffn_pre, g_ffn_post=g_ffn_post,
             w_in=w_in, conv_dw=conv_dw, conv_b=conv_b, conv_ln_g=conv_ln_g, conv_ln_b=conv_ln_b,
             mu_shift=mu_shift, w0=w0, w2=w2, a0=a0, a2=a2, g2=g2, k_k=k_k, k_a=k_a, r_k=r_k,
             lnx_g=lnx_g, lnx_b=lnx_b, w_out=w_out, w_gate=w_gate, w_up=w_up, w_down=w_down)
    depth = w_mod.shape[0]
    bp, tp, _ = x_prompt.shape
    bs, ts, _ = x_sample.shape
    mod = _modulation(jnp.concatenate([c_prompt, c_sample], axis=0), w_mod, b_mod)
    mod = mod.reshape(depth, bp + bs, 6, D_MODEL)
    zero_conv = jnp.zeros((bp, CONV_BUF, D_CONV), F32)
    zero_shift = jnp.zeros((bp, D_SHIFT), F32)
    zero_wkv = jnp.zeros((bp, N_HEADS, HEAD_DIM, HEAD_DIM), F32)
    yp, ys = x_prompt, x_sample
    cp, sp, wp, cs, ss, wsm = [], [], [], [], [], []
    for l in range(depth):
        p = _layer_params(l, w)
        yp, b1, s1, k1 = _layer(yp, mod[l, :bp], zero_conv, zero_shift, zero_wkv, p, _tiles(tp))
        ys, b2, s2, k2 = _layer(ys, mod[l, bp:], cache_conv[l], state_shift[l], state_wkv[l], p,
                                _tiles(ts))
        cp.append(b1); sp.append(s1); wp.append(k1)
        cs.append(b2); ss.append(s2); wsm.append(k2)
    return (yp, ys, jnp.stack(cp), jnp.stack(sp), jnp.stack(wp),
            jnp.stack(cs), jnp.stack(ss), jnp.stack(wsm))
```

```python
import functools
import math

import jax
import jax.numpy as jnp
from jax import lax
from jax.experimental import pallas as pl
from jax.experimental.pallas import tpu as pltpu

F32 = jnp.float32
BF16 = jnp.bfloat16

D_MODEL = 1024
D_CONV = 512
D_RWKV = 512
HEAD_DIM = 64
N_HEADS = 8
CONV_WIDTH = 31
CONV_BUF = 30
LORA_W = 64
LORA_A = 64
LORA_G = 128
D_SHIFT = 3 * D_RWKV + LORA_W + LORA_A + LORA_G
D_IN = 2 * D_CONV + D_SHIFT
RMS_EPS = 1e-6
LN_EPS = 1e-5
GN_EPS = 64e-5
L2_EPS = 1e-12

CHUNK = 64
QUAD = 4 * HEAD_DIM
CONV_HALO = 32
VMEM_LIMIT = 56 * 1024 * 1024


def _dot(a, b):
    return jnp.dot(a, b, preferred_element_type=F32)


def _dot_nt(a, b):
    return lax.dot_general(a, b, (((1,), (1,)), ((), ())), preferred_element_type=F32)


def _dot_tn(a, b):
    return lax.dot_general(a, b, (((0,), (0,)), ((), ())), preferred_element_type=F32)


def _dot_hilo_lhs(m, x):
    hi = x.astype(BF16)
    lo = (x - hi.astype(F32)).astype(BF16)
    return _dot(m, hi) + _dot(m, lo)


def _head_sum(x, ones):
    xb = x.astype(BF16)
    return jnp.concatenate(
        [_dot(xb[:, q * QUAD:(q + 1) * QUAD], ones) for q in range(D_RWKV // QUAD)], axis=1)


def _sigmoid(x):
    return 1.0 / (1.0 + jnp.exp(-x))


def _rms(x, g):
    return x * lax.rsqrt(jnp.mean(x * x, axis=-1, keepdims=True) + RMS_EPS) * g


def _mod_kernel(c_ref, w_ref, b_ref, o_ref):
    c = c_ref[...]
    s = c * _sigmoid(c)
    o_ref[0] = jnp.dot(s, w_ref[0], preferred_element_type=F32,
                       precision=lax.Precision.HIGHEST) + b_ref[0]


def _modulation(c_all, w_mod, b_mod):
    L = w_mod.shape[0]
    bc = c_all.shape[0]
    n = w_mod.shape[2]
    tn = 1024
    return pl.pallas_call(
        _mod_kernel,
        grid=(L, n // tn),
        in_specs=[
            pl.BlockSpec((bc, D_MODEL), lambda l, j: (0, 0)),
            pl.BlockSpec((1, D_MODEL, tn), lambda l, j: (l, 0, j)),
            pl.BlockSpec((1, 1, tn), lambda l, j: (l, 0, j)),
        ],
        out_specs=pl.BlockSpec((1, bc, tn), lambda l, j: (l, 0, j)),
        out_shape=jax.ShapeDtypeStruct((L, bc, n), F32),
        compiler_params=pltpu.CompilerParams(
            dimension_semantics=("arbitrary", "arbitrary"), vmem_limit_bytes=VMEM_LIMIT),
        name="modulation",
    )(c_all, w_mod, b_mod.reshape(L, 1, n))


def _inproj_kernel(x_ref, mod_ref, g_ref, w_ref, glu_ref, z_ref):
    x = x_ref[0]
    m = mod_ref[0]
    h = _rms(x, g_ref[...]) * (1.0 + m[1:2]) + m[0:1]
    hb = h.astype(BF16)
    u = _dot(hb, w_ref[:, :2 * D_CONV])
    glu_ref[0] = u[:, :D_CONV] * _sigmoid(u[:, D_CONV:])
    z_ref[0] = _dot(hb, w_ref[:, 2 * D_CONV:])


def _inproj(x, mod, g, w_in_bf, tm):
    B, T, _ = x.shape
    return pl.pallas_call(
        _inproj_kernel,
        grid=(B, T // tm),
        in_specs=[
            pl.BlockSpec((1, tm, D_MODEL), lambda b, t: (b, t, 0)),
            pl.BlockSpec((1, 6, D_MODEL), lambda b, t: (b, 0, 0)),
            pl.BlockSpec((1, D_MODEL), lambda b, t: (0, 0)),
            pl.BlockSpec((D_MODEL, D_IN), lambda b, t: (0, 0)),
        ],
        out_specs=[
            pl.BlockSpec((1, tm, D_CONV), lambda b, t: (b, t, 0)),
            pl.BlockSpec((1, tm, D_SHIFT), lambda b, t: (b, t, 0)),
        ],
        out_shape=[
            jax.ShapeDtypeStruct((B, T, D_CONV), F32),
            jax.ShapeDtypeStruct((B, T, D_SHIFT), F32),
        ],
        compiler_params=pltpu.CompilerParams(
            dimension_semantics=("arbitrary", "arbitrary"), vmem_limit_bytes=VMEM_LIMIT),
        name="inproj",
    )(x, mod, g, w_in_bf)


def _conv_kernel(glu_ref, buf_ref, dw_ref, cb_ref, lg_ref, lb_ref, out_ref, nbuf_ref, win_ref,
                 *, tt, rb):
    t = pl.program_id(1)
    pad = CONV_HALO - CONV_BUF

    @pl.when(t == 0)
    def _():
        win_ref[0:pad, :] = jnp.zeros((pad, D_CONV), F32)
        win_ref[pad:CONV_HALO, :] = buf_ref[0]

    win_ref[CONV_HALO:CONV_HALO + tt, :] = glu_ref[0]

    for r in range(tt // rb):
        acc = jnp.broadcast_to(cb_ref[...], (rb, D_CONV))
        for j in range(CONV_WIDTH):
            lo = r * rb + pad + j
            acc = acc + win_ref[lo:lo + rb, :] * dw_ref[j:j + 1, :]
        mu = jnp.mean(acc, axis=-1, keepdims=True)
        d = acc - mu
        var = jnp.mean(d * d, axis=-1, keepdims=True)
        y = d * lax.rsqrt(var + LN_EPS) * lg_ref[...] + lb_ref[...]
        out_ref[0, r * rb:(r + 1) * rb, :] = y * _sigmoid(y)

    win_ref[0:CONV_HALO, :] = win_ref[tt:tt + CONV_HALO, :]

    @pl.when(t == pl.num_programs(1) - 1)
    def _():
        nbuf_ref[0] = win_ref[pad:CONV_HALO, :]


def _conv_mix(glu, buf, dw, cb, lg, lb, tt):
    B, T, _ = glu.shape
    rb = min(tt, 64)
    row = lambda b, t: (0, 0)
    return pl.pallas_call(
        functools.partial(_conv_kernel, tt=tt, rb=rb),
        grid=(B, T // tt),
        in_specs=[
            pl.BlockSpec((1, tt, D_CONV), lambda b, t: (b, t, 0)),
            pl.BlockSpec((1, CONV_BUF, D_CONV), lambda b, t: (b, 0, 0)),
            pl.BlockSpec((CONV_WIDTH, D_CONV), row),
            pl.BlockSpec((1, D_CONV), row),
            pl.BlockSpec((1, D_CONV), row),
            pl.BlockSpec((1, D_CONV), row),
        ],
        out_specs=[
            pl.BlockSpec((1, tt, D_CONV), lambda b, t: (b, t, 0)),
            pl.BlockSpec((1, CONV_BUF, D_CONV), lambda b, t: (b, 0, 0)),
        ],
        out_shape=[
            jax.ShapeDtypeStruct((B, T, D_CONV), F32),
            jax.ShapeDtypeStruct((B, CONV_BUF, D_CONV), F32),
        ],
        scratch_shapes=[pltpu.VMEM((CONV_HALO + tt, D_CONV), F32)],
        compiler_params=pltpu.CompilerParams(
            dimension_semantics=("arbitrary", "arbitrary"), vmem_limit_bytes=VMEM_LIMIT),
        name="conv_mix",
    )(glu, buf, dw, cb, lg, lb)


def _block_diag(x, bdmask):
    return jnp.where(bdmask, jnp.concatenate([x, x, x, x], axis=0), jnp.zeros((), x.dtype))


def _rwkv_kernel(z_ref, sprev_ref, s0_ref, mu_ref, w0_ref, wa_ref, a0_ref, g2_ref, kk_ref, ka_ref,
                 rk_ref, lg_ref, lb_ref, ones_ref, tri_ref,
                 out_ref, nshift_ref, sout_ref,
                 carry_ref, bds_ref, ra_ref, aa_ref, bb_ref, kt_ref, bh_ref, kh_ref, v_ref,
                 pc_ref, y_ref, tm_ref, np_ref, arb_ref, aak_ref, ark_ref, tmb_ref, av_ref, w_ref,
                 qm_ref, bdm_ref, bdn_ref, *, tt):
    t = pl.program_id(1)
    nq = D_RWKV // QUAD
    rowq = lax.broadcasted_iota(jnp.int32, (QUAD, QUAD), 0)
    colq = lax.broadcasted_iota(jnp.int32, (QUAD, QUAD), 1)
    bdmask = (rowq // HEAD_DIM) == (colq // HEAD_DIM)
    eye = rowq == colq

    first = t == 0

    z = z_ref[0]
    rows = lax.broadcasted_iota(jnp.int32, (tt, 1), 0)
    last_row = jnp.where(first, sprev_ref[0], carry_ref[...])
    z_prev = jnp.where(rows == 0, last_row, pltpu.roll(z, 1, 0))
    carry_ref[...] = z[tt - 1:tt, :]
    nshift_ref[0] = z[tt - 1:tt, :]
    zm = z + (z_prev - z) * mu_ref[...]
    r = zm[:, 0:D_RWKV]
    k = zm[:, D_RWKV:2 * D_RWKV]
    v = zm[:, 2 * D_RWKV:3 * D_RWKV]
    zwa = zm[:, 3 * D_RWKV:3 * D_RWKV + LORA_W + LORA_A]
    zg = zm[:, 3 * D_RWKV + LORA_W + LORA_A:]
    lane = lax.broadcasted_iota(jnp.int32, zwa.shape, 1)
    zwa = jnp.where(lane < LORA_W, jnp.tanh(zwa), zwa)
    wa = _dot(zwa.astype(BF16), wa_ref[...])
    lw = -math.exp(-0.5) * _sigmoid(w0_ref[...] + wa[:, :D_RWKV])
    a = _sigmoid(a0_ref[...] + wa[:, D_RWKV:])
    g = _dot(_sigmoid(zg).astype(BF16), g2_ref[...])

    ones = ones_ref[...]
    kk = k * kk_ref[...]
    kk = kk * lax.rsqrt(_head_sum(kk * kk, ones) + L2_EPS)
    k = k * (1.0 + (a - 1.0) * ka_ref[...])
    b = kk * a
    bonus = _head_sum(r * k * rk_ref[...], ones) * v

    cum = _dot_hilo_lhs(tri_ref[...], lw)
    nchunk = tt // CHUNK
    cend = cum.reshape(nchunk, CHUNK, D_RWKV)[:, CHUNK - 1:CHUNK, :]
    cend = jnp.broadcast_to(cend, (nchunk, CHUNK, D_RWKV)).reshape(tt, D_RWKV)
    e_neg = jnp.exp(-cum)
    e_end = jnp.exp(cend - cum)
    ra_ref[...] = (r * jnp.exp(cum)).astype(BF16)
    aa_ref[...] = (-kk * jnp.exp(cum - lw)).astype(BF16)
    bb_ref[...] = (b * e_neg).astype(BF16)
    kt_ref[...] = (k * e_neg).astype(BF16)
    bh_ref[...] = (b * e_end).astype(BF16)
    kh_ref[...] = (k * e_end).astype(BF16)
    v_ref[...] = v.astype(BF16)
    pc_ref[...] = jnp.exp(cend)

    rowc = lax.broadcasted_iota(jnp.int32, (CHUNK, QUAD), 0)
    colc = lax.broadcasted_iota(jnp.int32, (CHUNK, QUAD), 1) % CHUNK
    strict = colc < rowc
    incl = colc <= rowc
    ident = jnp.where(colc == rowc, 1.0, 0.0)

    def mm(x, ybd):
        return _dot(x.astype(BF16), ybd)

    ni = nchunk * nq
    inst = [(c * CHUNK, q * QUAD) for c in range(nchunk) for q in range(nq)]

    def tile(ref, i):
        r0, c0 = inst[i]
        return ref[r0:r0 + CHUNK, c0:c0 + QUAD]

    for i in range(ni):
        lhs = jnp.concatenate([tile(aa_ref, i), tile(ra_ref, i)], axis=0)
        sb = _dot_nt(lhs, _block_diag(tile(bb_ref, i), bdmask))
        sk = _dot_nt(lhs, _block_diag(tile(kt_ref, i), bdmask))
        a_ab = jnp.where(strict, sb[:CHUNK], 0.0)
        tm_ref[i] = ident + a_ab
        np_ref[i] = a_ab.astype(BF16)
        arb_ref[i] = jnp.where(incl, sb[CHUNK:], 0.0).astype(BF16)
        aak_ref[i] = jnp.where(strict, sk[:CHUNK], 0.0).astype(BF16)
        ark_ref[i] = jnp.where(incl, sk[CHUNK:], 0.0).astype(BF16)
    for i in range(ni):
        n1 = np_ref[i]
        np_ref[i] = _dot(n1, _block_diag(n1, bdmask)).astype(BF16)
    for _ in range(4):
        for i in range(ni):
            npow = np_ref[i]
            tm = tm_ref[i]
            res = _dot(jnp.concatenate([tm.astype(BF16), npow], axis=0), _block_diag(npow, bdmask))
            tm_ref[i] = tm + res[:CHUNK]
            np_ref[i] = res[CHUNK:].astype(BF16)
    for i in range(ni):
        tm = tm_ref[i]
        tmb_ref[i] = (tm + mm(tm, _block_diag(np_ref[i], bdmask))).astype(BF16)
        av_ref[i] = _dot(aak_ref[i], _block_diag(tile(v_ref, i), bdmask)).astype(BF16)
    for i in range(ni):
        rhs = jnp.concatenate([_block_diag(tile(aa_ref, i), bdmask), _block_diag(av_ref[i], bdmask)],
                              axis=1)
        w_ref[i] = _dot(tmb_ref[i], rhs).astype(BF16)
    for i in range(ni):
        r0, c0 = inst[i]
        w12 = w_ref[i]
        rhs = jnp.concatenate([_block_diag(w12[:, :QUAD], bdmask), _block_diag(w12[:, QUAD:], bdmask)],
                              axis=1)
        res = _dot(arb_ref[i], rhs)
        qm_ref[i] = (tile(ra_ref, i).astype(F32) + res[:, :QUAD]).astype(BF16)
        y_ref[r0:r0 + CHUNK, c0:c0 + QUAD] = (
            res[:, QUAD:] + _dot(ark_ref[i], _block_diag(tile(v_ref, i), bdmask)))
        res = _dot_tn(tile(bh_ref, i), w12)
        pc = pc_ref[r0:r0 + 1, c0:c0 + QUAD]
        bdm_ref[i] = (jnp.where(eye, pc, 0.0) + jnp.where(bdmask, res[:, :QUAD], 0.0)).astype(BF16)
        bdn_ref[i] = jnp.where(bdmask, res[:, QUAD:] + _dot_tn(tile(kh_ref, i), tile(v_ref, i)), 0.0)
    for q in range(nq):
        s = jnp.where(first, _block_diag(s0_ref[0, q], bdmask), bds_ref[q])
        for c in range(nchunk):
            i = c * nq + q
            r0, c0 = inst[i]
            sb16 = s.astype(BF16)
            y_ref[r0:r0 + CHUNK, c0:c0 + QUAD] += _dot(qm_ref[i], sb16)
            s = _dot(bdm_ref[i], sb16) + bdn_ref[i]
        bds_ref[q] = s
        sout_ref[0, q] = (s[0:64] + s[64:128]) + (s[128:192] + s[192:256])

    y = y_ref[...]
    mean = _head_sum(y, ones) * (1.0 / HEAD_DIM)
    d = y - mean
    var = _head_sum(d * d, ones) * (1.0 / HEAD_DIM)
    yn = d * lax.rsqrt(var + GN_EPS) * lg_ref[...] + lb_ref[...]
    out_ref[0] = (yn + bonus) * g


def _rwkv_mix(z, shift_prev, s0q, mu, w0, wa_bf, a0, g2_bf, k_k, k_a, r_k, lnx_g, lnx_b, tt):
    B, T, _ = z.shape
    nq = D_RWKV // QUAD
    ni = (tt // CHUNK) * nq
    idx = jnp.arange(QUAD) // HEAD_DIM
    ones = (idx[:, None] == idx[None, :]).astype(BF16)
    ti = jnp.arange(tt)
    tri = ((ti[:, None] // CHUNK == ti[None, :] // CHUNK) & (ti[None, :] <= ti[:, None])).astype(BF16)
    row = lambda b, t: (0, 0)
    vec = pl.BlockSpec((1, D_RWKV), row)
    return pl.pallas_call(
        functools.partial(_rwkv_kernel, tt=tt),
        grid=(B, T // tt),
        in_specs=[
            pl.BlockSpec((1, tt, D_SHIFT), lambda b, t: (b, t, 0)),
            pl.BlockSpec((1, 1, D_SHIFT), lambda b, t: (b, 0, 0)),
            pl.BlockSpec((1, nq, HEAD_DIM, QUAD), lambda b, t: (b, 0, 0, 0)),
            pl.BlockSpec((1, D_SHIFT), row),
            vec,
            pl.BlockSpec((LORA_W + LORA_A, 2 * D_RWKV), row),
            vec,
            pl.BlockSpec((LORA_G, D_RWKV), row),
            vec, vec, vec, vec, vec,
            pl.BlockSpec((QUAD, QUAD), row),
            pl.BlockSpec((tt, tt), row),
        ],
        out_specs=[
            pl.BlockSpec((1, tt, D_RWKV), lambda b, t: (b, t, 0)),
            pl.BlockSpec((1, 1, D_SHIFT), lambda b, t: (b, 0, 0)),
            pl.BlockSpec((1, nq, HEAD_DIM, QUAD), lambda b, t: (b, 0, 0, 0)),
        ],
        out_shape=[
            jax.ShapeDtypeStruct((B, T, D_RWKV), F32),
            jax.ShapeDtypeStruct((B, 1, D_SHIFT), F32),
            jax.ShapeDtypeStruct((B, nq, HEAD_DIM, QUAD), F32),
        ],
        scratch_shapes=[
            pltpu.VMEM((1, D_SHIFT), F32),
            pltpu.VMEM((nq, QUAD, QUAD), F32),
            pltpu.VMEM((tt, D_RWKV), BF16),
            pltpu.VMEM((tt, D_RWKV), BF16),
            pltpu.VMEM((tt, D_RWKV), BF16),
            pltpu.VMEM((tt, D_RWKV), BF16),
            pltpu.VMEM((tt, D_RWKV), BF16),
            pltpu.VMEM((tt, D_RWKV), BF16),
            pltpu.VMEM((tt, D_RWKV), BF16),
            pltpu.VMEM((tt, D_RWKV), F32),
            pltpu.VMEM((tt, D_RWKV), F32),
            pltpu.VMEM((ni, CHUNK, QUAD), F32),
            pltpu.VMEM((ni, CHUNK, QUAD), BF16),
            pltpu.VMEM((ni, CHUNK, QUAD), BF16),
            pltpu.VMEM((ni, CHUNK, QUAD), BF16),
            pltpu.VMEM((ni, CHUNK, QUAD), BF16),
            pltpu.VMEM((ni, CHUNK, QUAD), BF16),
            pltpu.VMEM((ni, CHUNK, QUAD), BF16),
            pltpu.VMEM((ni, CHUNK, 2 * QUAD), BF16),
            pltpu.VMEM((ni, CHUNK, QUAD), BF16),
            pltpu.VMEM((ni, QUAD, QUAD), BF16),
            pltpu.VMEM((ni, QUAD, QUAD), F32),
        ],
        compiler_params=pltpu.CompilerParams(
            dimension_semantics=("arbitrary", "arbitrary"), vmem_limit_bytes=VMEM_LIMIT),
        name="rwkv_mix",
    )(z, shift_prev, s0q, mu, w0, wa_bf, a0, g2_bf, k_k, k_a, r_k, lnx_g, lnx_b, ones, tri)


def _out_ffn_kernel(x_ref, co_ref, ro_ref, mod_ref, gpost_ref, gfpre_ref, gfpost_ref,
                    wo_ref, wg_ref, wu_ref, wd_ref, y_ref):
    x = x_ref[0]
    m = mod_ref[0]
    mix = _dot(co_ref[0].astype(BF16), wo_ref[:D_CONV, :]) + _dot(ro_ref[0].astype(BF16), wo_ref[D_CONV:, :])
    x = x + (1.0 + m[2:3]) * _rms(mix, gpost_ref[...])
    h = (_rms(x, gfpre_ref[...]) * (1.0 + m[4:5]) + m[3:4]).astype(BF16)
    gate = _dot(h, wg_ref[...])
    up = _dot(h, wu_ref[...])
    act = (gate * _sigmoid(gate) * up).astype(BF16)
    f = _dot(act, wd_ref[...])
    y_ref[0] = x + (1.0 + m[5:6]) * _rms(f, gfpost_ref[...])


def _out_ffn(x, conv_out, rwkv_out, mod, g_post, g_fpre, g_fpost, wo, wg, wu, wd, tm):
    B, T, _ = x.shape
    d_ff = wg.shape[1]
    row = lambda b, t: (0, 0)
    once = dict(pipeline_mode=pl.Buffered(1))
    return pl.pallas_call(
        _out_ffn_kernel,
        grid=(B, T // tm),
        in_specs=[
            pl.BlockSpec((1, tm, D_MODEL), lambda b, t: (b, t, 0)),
            pl.BlockSpec((1, tm, D_CONV), lambda b, t: (b, t, 0)),
            pl.BlockSpec((1, tm, D_RWKV), lambda b, t: (b, t, 0)),
            pl.BlockSpec((1, 6, D_MODEL), lambda b, t: (b, 0, 0)),
            pl.BlockSpec((1, D_MODEL), row),
            pl.BlockSpec((1, D_MODEL), row),
            pl.BlockSpec((1, D_MODEL), row),
            pl.BlockSpec((D_MODEL, D_MODEL), row, **once),
            pl.BlockSpec((D_MODEL, d_ff), row, **once),
            pl.BlockSpec((D_MODEL, d_ff), row, **once),
            pl.BlockSpec((d_ff, D_MODEL), row, **once),
        ],
        out_specs=pl.BlockSpec((1, tm, D_MODEL), lambda b, t: (b, t, 0)),
        out_shape=jax.ShapeDtypeStruct((B, T, D_MODEL), F32),
        compiler_params=pltpu.CompilerParams(
            dimension_semantics=("arbitrary", "arbitrary"), vmem_limit_bytes=VMEM_LIMIT),
        name="out_ffn",
    )(x, conv_out, rwkv_out, mod, g_post, g_fpre, g_fpost, wo, wg, wu, wd)


def _state_to_quads(s):
    B = s.shape[0]
    s = s.reshape(B, N_HEADS // 4, 4, HEAD_DIM, HEAD_DIM)
    return s.transpose(0, 1, 4, 2, 3).reshape(B, N_HEADS // 4, HEAD_DIM, QUAD)


def _quads_to_state(sq):
    B = sq.shape[0]
    s = sq.reshape(B, N_HEADS // 4, HEAD_DIM, 4, HEAD_DIM)
    return s.transpose(0, 1, 3, 4, 2).reshape(B, N_HEADS, HEAD_DIM, HEAD_DIM)


def _layer(x, mod, conv_buf, shift_prev, wkv, p, tiles):
    tm, tc, tr = tiles
    glu, z = _inproj(x, mod, p["g_mix_pre"], p["w_in"], tm)
    conv_out, new_buf = _conv_mix(glu, conv_buf, p["conv_dw"], p["conv_b"], p["conv_ln_g"],
                                  p["conv_ln_b"], tc)
    rwkv_out, new_shift, sq = _rwkv_mix(
        z, shift_prev[:, None, :], _state_to_quads(wkv), p["mu_shift"], p["w0"], p["wa"], p["a0"],
        p["g2"], p["k_k"], p["k_a"], p["r_k"], p["lnx_g"], p["lnx_b"], tr)
    y = _out_ffn(x, conv_out, rwkv_out, mod, p["g_mix_post"], p["g_ffn_pre"], p["g_ffn_post"],
                 p["w_out"], p["w_gate"], p["w_up"], p["w_down"], tm)
    return y, new_buf, new_shift[:, 0, :], _quads_to_state(sq)


def _layer_params(l, w):
    row = lambda a: a[l].reshape(1, -1)
    wa = jnp.zeros((LORA_W + LORA_A, 2 * D_RWKV), F32)
    wa = wa.at[:LORA_W, :D_RWKV].set(w["w2"][l]).at[LORA_W:, D_RWKV:].set(w["a2"][l])
    return {
        "g_mix_pre": row(w["g_mix_pre"]), "g_mix_post": row(w["g_mix_post"]),
        "g_ffn_pre": row(w["g_ffn_pre"]), "g_ffn_post": row(w["g_ffn_post"]),
        "w_in": w["w_in"][l].astype(BF16),
        "conv_dw": w["conv_dw"][l], "conv_b": row(w["conv_b"]),
        "conv_ln_g": row(w["conv_ln_g"]), "conv_ln_b": row(w["conv_ln_b"]),
        "mu_shift": row(w["mu_shift"]), "w0": row(w["w0"]), "wa": wa.astype(BF16),
        "a0": row(w["a0"]), "g2": w["g2"][l].astype(BF16),
        "k_k": row(w["k_k"]), "k_a": row(w["k_a"]), "r_k": row(w["r_k"]),
        "lnx_g": row(w["lnx_g"]), "lnx_b": row(w["lnx_b"]),
        "w_out": w["w_out"][l].astype(BF16), "w_gate": w["w_gate"][l].astype(BF16),
        "w_up": w["w_up"][l].astype(BF16), "w_down": w["w_down"][l].astype(BF16),
    }


def _tiles(T):
    tm = min(T, 512)
    tc = min(T, 256)
    tr = min(T, 256)
    return tm, tc, tr


def kernel(x_prompt, x_sample, cache_conv, state_shift, state_wkv, c_prompt, c_sample, w_mod, b_mod, g_mix_pre, g_mix_post, g_ffn_pre, g_ffn_post, w_in, conv_dw, conv_b, conv_ln_g, conv_ln_b, mu_shift, w0, w2, a0, a2, g2, k_k, k_a, r_k, lnx_g, lnx_b, w_out, w_gate, w_up, w_down):
    w = dict(g_mix_pre=g_mix_pre, g_mix_post=g_mix_post, g_ffn_pre=g_ffn_pre, g_ffn_post=g_ffn_post,
             w_in=w_in, conv_dw=conv_dw, conv_b=conv_b, conv_ln_g=conv_ln_g, conv_ln_b=conv_ln_b,
             mu_shift=mu_shift, w0=w0, w2=w2, a0=a0, a2=a2, g2=g2, k_k=k_k, k_a=k_a, r_k=r_k,
             lnx_g=lnx_g, lnx_b=lnx_b, w_out=w_out, w_gate=w_gate, w_up=w_up, w_down=w_down)
    depth = w_mod.shape[0]
    bp, tp, _ = x_prompt.shape
    bs, ts, _ = x_sample.shape
    mod = _modulation(jnp.concatenate([c_prompt, c_sample], axis=0), w_mod, b_mod)
    mod = mod.reshape(depth, bp + bs, 6, D_MODEL)
    zero_conv = jnp.zeros((bp, CONV_BUF, D_CONV), F32)
    zero_shift = jnp.zeros((bp, D_SHIFT), F32)
    zero_wkv = jnp.zeros((bp, N_HEADS, HEAD_DIM, HEAD_DIM), F32)
    yp, ys = x_prompt, x_sample
    cp, sp, wp, cs, ss, wsm = [], [], [], [], [], []
    for l in range(depth):
        p = _layer_params(l, w)
        yp, b1, s1, k1 = _layer(yp, mod[l, :bp], zero_conv, zero_shift, zero_wkv, p, _tiles(tp))
        ys, b2, s2, k2 = _layer(ys, mod[l, bp:], cache_conv[l], state_shift[l], state_wkv[l], p,
                                _tiles(ts))
        cp.append(b1); sp.append(s1); wp.append(k1)
        cs.append(b2); ss.append(s2); wsm.append(k2)
    return (yp, ys, jnp.stack(cp), jnp.stack(sp), jnp.stack(wp),
            jnp.stack(cs), jnp.stack(ss), jnp.stack(wsm))
```

```python
import functools
import math

import jax
import jax.numpy as jnp
from jax import lax
from jax.experimental import pallas as pl
from jax.experimental.pallas import tpu as pltpu

F32 = jnp.float32
BF16 = jnp.bfloat16

D_MODEL = 1024
D_CONV = 512
D_RWKV = 512
HEAD_DIM = 64
N_HEADS = 8
CONV_WIDTH = 31
CONV_BUF = 30
LORA_W = 64
LORA_A = 64
LORA_G = 128
D_SHIFT = 3 * D_RWKV + LORA_W + LORA_A + LORA_G
D_IN = 2 * D_CONV + D_SHIFT
RMS_EPS = 1e-6
LN_EPS = 1e-5
GN_EPS = 64e-5
L2_EPS = 1e-12

CHUNK = 64
QUAD = 4 * HEAD_DIM
CONV_HALO = 32
VMEM_LIMIT = 56 * 1024 * 1024


def _dot(a, b):
    return jnp.dot(a, b, preferred_element_type=F32)


def _dot_nt(a, b):
    return lax.dot_general(a, b, (((1,), (1,)), ((), ())), preferred_element_type=F32)


def _dot_tn(a, b):
    return lax.dot_general(a, b, (((0,), (0,)), ((), ())), preferred_element_type=F32)


def _dot_hilo_lhs(m, x):
    hi = x.astype(BF16)
    lo = (x - hi.astype(F32)).astype(BF16)
    return _dot(m, hi) + _dot(m, lo)


def _head_sum(x, ones):
    xb = x.astype(BF16)
    return jnp.concatenate(
        [_dot(xb[:, q * QUAD:(q + 1) * QUAD], ones) for q in range(D_RWKV // QUAD)], axis=1)


def _sigmoid(x):
    return 1.0 / (1.0 + jnp.exp(-x))


def _rms(x, g):
    return x * lax.rsqrt(jnp.mean(x * x, axis=-1, keepdims=True) + RMS_EPS) * g


def _mod_kernel(c_ref, w_ref, b_ref, o_ref):
    c = c_ref[...]
    s = c * _sigmoid(c)
    o_ref[0] = jnp.dot(s, w_ref[0], preferred_element_type=F32,
                       precision=lax.Precision.HIGHEST) + b_ref[0]


def _modulation(c_all, w_mod, b_mod):
    L = w_mod.shape[0]
    bc = c_all.shape[0]
    n = w_mod.shape[2]
    tn = 1024
    return pl.pallas_call(
        _mod_kernel,
        grid=(L, n // tn),
        in_specs=[
            pl.BlockSpec((bc, D_MODEL), lambda l, j: (0, 0)),
            pl.BlockSpec((1, D_MODEL, tn), lambda l, j: (l, 0, j)),
            pl.BlockSpec((1, 1, tn), lambda l, j: (l, 0, j)),
        ],
        out_specs=pl.BlockSpec((1, bc, tn), lambda l, j: (l, 0, j)),
        out_shape=jax.ShapeDtypeStruct((L, bc, n), F32),
        compiler_params=pltpu.CompilerParams(
            dimension_semantics=("arbitrary", "arbitrary"), vmem_limit_bytes=VMEM_LIMIT),
        name="modulation",
    )(c_all, w_mod, b_mod.reshape(L, 1, n))


def _inproj_kernel(x_ref, mod_ref, g_ref, w_ref, glu_ref, z_ref):
    x = x_ref[0]
    m = mod_ref[0]
    h = _rms(x, g_ref[...]) * (1.0 + m[1:2]) + m[0:1]
    hb = h.astype(BF16)
    u = _dot(hb, w_ref[:, :2 * D_CONV])
    glu_ref[0] = u[:, :D_CONV] * _sigmoid(u[:, D_CONV:])
    z_ref[0] = _dot(hb, w_ref[:, 2 * D_CONV:])


def _inproj(x, mod, g, w_in_bf, tm):
    B, T, _ = x.shape
    return pl.pallas_call(
        _inproj_kernel,
        grid=(B, T // tm),
        in_specs=[
            pl.BlockSpec((1, tm, D_MODEL), lambda b, t: (b, t, 0)),
            pl.BlockSpec((1, 6, D_MODEL), lambda b, t: (b, 0, 0)),
            pl.BlockSpec((1, D_MODEL), lambda b, t: (0, 0)),
            pl.BlockSpec((D_MODEL, D_IN), lambda b, t: (0, 0)),
        ],
        out_specs=[
            pl.BlockSpec((1, tm, D_CONV), lambda b, t: (b, t, 0)),
            pl.BlockSpec((1, tm, D_SHIFT), lambda b, t: (b, t, 0)),
        ],
        out_shape=[
            jax.ShapeDtypeStruct((B, T, D_CONV), F32),
            jax.ShapeDtypeStruct((B, T, D_SHIFT), F32),
        ],
        compiler_params=pltpu.CompilerParams(
            dimension_semantics=("arbitrary", "arbitrary"), vmem_limit_bytes=VMEM_LIMIT),
        name="inproj",
    )(x, mod, g, w_in_bf)


def _conv_kernel(glu_ref, buf_ref, dw_ref, cb_ref, lg_ref, lb_ref, out_ref, nbuf_ref, win_ref,
                 *, tt, rb):
    first = pl.program_id(1) == 0
    pad = CONV_HALO - CONV_BUF
    half = D_CONV // 2

    win_ref[0:CONV_HALO, :] = jnp.where(first, buf_ref[0], win_ref[0:CONV_HALO, :])
    win_ref[CONV_HALO:CONV_HALO + tt, :] = glu_ref[0]

    for r in range(tt // rb):
        base = r * rb
        accs = []
        for h in range(2):
            cols = slice(h * half, (h + 1) * half)
            acc = jnp.broadcast_to(cb_ref[:, cols], (rb, half))
            for s in range(8):
                part = None
                for a in range((pad + CONV_WIDTH + 7) // 8):
                    j = 8 * a + s
                    if j < pad or j >= pad + CONV_WIDTH:
                        continue
                    lo = base + 8 * a
                    term = win_ref[lo:lo + rb + (8 if s else 0), cols] * dw_ref[j - pad:j - pad + 1, cols]
                    part = term if part is None else part + term
                acc = acc + part[s:s + rb]
            accs.append(acc)
        acc = jnp.concatenate(accs, axis=1)
        mu = jnp.mean(acc, axis=-1, keepdims=True)
        d = acc - mu
        var = jnp.mean(d * d, axis=-1, keepdims=True)
        y = d * lax.rsqrt(var + LN_EPS) * lg_ref[...] + lb_ref[...]
        out_ref[0, base:base + rb, :] = y * _sigmoid(y)

    tail = win_ref[tt:tt + CONV_HALO, :]
    win_ref[0:CONV_HALO, :] = tail
    nbuf_ref[0] = tail


def _conv_mix(glu, buf, dw, cb, lg, lb, tt):
    B, T, _ = glu.shape
    rb = min(tt, 64)
    pad = CONV_HALO - CONV_BUF
    buf = jnp.pad(buf, ((0, 0), (pad, 0), (0, 0)))
    row = lambda b, t: (0, 0)
    out, nbuf = pl.pallas_call(
        functools.partial(_conv_kernel, tt=tt, rb=rb),
        grid=(B, T // tt),
        in_specs=[
            pl.BlockSpec((1, tt, D_CONV), lambda b, t: (b, t, 0)),
            pl.BlockSpec((1, CONV_HALO, D_CONV), lambda b, t: (b, 0, 0)),
            pl.BlockSpec((CONV_WIDTH, D_CONV), row),
            pl.BlockSpec((1, D_CONV), row),
            pl.BlockSpec((1, D_CONV), row),
            pl.BlockSpec((1, D_CONV), row),
        ],
        out_specs=[
            pl.BlockSpec((1, tt, D_CONV), lambda b, t: (b, t, 0)),
            pl.BlockSpec((1, CONV_HALO, D_CONV), lambda b, t: (b, 0, 0)),
        ],
        out_shape=[
            jax.ShapeDtypeStruct((B, T, D_CONV), F32),
            jax.ShapeDtypeStruct((B, CONV_HALO, D_CONV), F32),
        ],
        scratch_shapes=[pltpu.VMEM((CONV_HALO + tt, D_CONV), F32)],
        compiler_params=pltpu.CompilerParams(
            dimension_semantics=("arbitrary", "arbitrary"), vmem_limit_bytes=VMEM_LIMIT),
        name="conv_mix",
    )(glu, buf, dw, cb, lg, lb)
    return out, nbuf[:, pad:, :]


def _block_diag(x):
    half = QUAD // 2
    lo = lax.broadcasted_iota(jnp.int32, (HEAD_DIM, half), 1) < HEAD_DIM
    zero = jnp.zeros((HEAD_DIM, half), x.dtype)
    xl, xr = x[:, :half], x[:, half:]
    rows = [
        jnp.concatenate([jnp.where(lo, xl, zero), zero], axis=1),
        jnp.concatenate([jnp.where(lo, zero, xl), zero], axis=1),
        jnp.concatenate([zero, jnp.where(lo, xr, zero)], axis=1),
        jnp.concatenate([zero, jnp.where(lo, zero, xr)], axis=1),
    ]
    return jnp.concatenate(rows, axis=0)


def _rwkv_kernel(z_ref, sprev_ref, s0_ref, mu_ref, w0_ref, wa_ref, a0_ref, g2_ref, kk_ref, ka_ref,
                 rk_ref, lg_ref, lb_ref, ones_ref, tri_ref,
                 out_ref, nshift_ref, sout_ref,
                 carry_ref, sq_ref, ra_ref, aa_ref, bb_ref, kt_ref, v_ref,
                 pc_ref, y_ref, tm_ref, np_ref, arb_ref, aak_ref, bht_ref, tmb_ref, av_ref,
                 nkv_ref, w_ref, qm_ref, mq_ref, nq_ref, *, tt):
    first = pl.program_id(1) == 0
    nq = D_RWKV // QUAD
    nchunk = tt // CHUNK

    z = z_ref[0]
    rows = lax.broadcasted_iota(jnp.int32, (tt, 1), 0)
    last_row = jnp.where(first, sprev_ref[0], carry_ref[...])
    z_prev = jnp.where(rows == 0, last_row, pltpu.roll(z, 1, 0))
    carry_ref[...] = z[tt - 1:tt, :]
    nshift_ref[0] = z[tt - 1:tt, :]
    zm = z + (z_prev - z) * mu_ref[...]
    r = zm[:, 0:D_RWKV]
    k = zm[:, D_RWKV:2 * D_RWKV]
    v = zm[:, 2 * D_RWKV:3 * D_RWKV]
    zwa = zm[:, 3 * D_RWKV:3 * D_RWKV + LORA_W + LORA_A]
    zg = zm[:, 3 * D_RWKV + LORA_W + LORA_A:]
    lane = lax.broadcasted_iota(jnp.int32, zwa.shape, 1)
    zwa = jnp.where(lane < LORA_W, jnp.tanh(zwa), zwa)
    wa = _dot(zwa.astype(BF16), wa_ref[...])
    lw = -math.exp(-0.5) * _sigmoid(w0_ref[...] + wa[:, :D_RWKV])
    a = _sigmoid(a0_ref[...] + wa[:, D_RWKV:])
    g = _dot(_sigmoid(zg).astype(BF16), g2_ref[...])

    ones = ones_ref[...]
    kk = k * kk_ref[...]
    kk = kk * lax.rsqrt(_head_sum(kk * kk, ones) + L2_EPS)
    k = k * (1.0 + (a - 1.0) * ka_ref[...])
    b = kk * a
    bonus = _head_sum(r * k * rk_ref[...], ones) * v

    cum = _dot_hilo_lhs(tri_ref[...], lw)
    e_neg = jnp.exp(-cum)
    ra_ref[...] = (r * jnp.exp(cum)).astype(BF16)
    aa_ref[...] = (-kk * jnp.exp(cum - lw)).astype(BF16)
    bb_ref[...] = (b * e_neg).astype(BF16)
    kt_ref[...] = (k * e_neg).astype(BF16)
    v_ref[...] = v.astype(BF16)
    for c in range(nchunk):
        pc_ref[c:c + 1, :] = jnp.exp(cum[(c + 1) * CHUNK - 1:(c + 1) * CHUNK, :])

    rowc = lax.broadcasted_iota(jnp.int32, (CHUNK, QUAD), 0)
    colc = lax.broadcasted_iota(jnp.int32, (CHUNK, QUAD), 1) % CHUNK
    strict = colc < rowc
    incl = colc <= rowc
    ident = jnp.where(colc == rowc, 1.0, 0.0)

    ni = nchunk * nq
    inst = [(c, c * CHUNK, q * QUAD) for c in range(nchunk) for q in range(nq)]

    def tile(ref, i):
        _, r0, c0 = inst[i]
        return ref[r0:r0 + CHUNK, c0:c0 + QUAD]

    for i in range(ni):
        c, r0, c0 = inst[i]
        pc = pc_ref[c:c + 1, c0:c0 + QUAD]
        lhs = jnp.concatenate([tile(aa_ref, i), tile(ra_ref, i), (ident * pc).astype(BF16)], axis=0)
        sb = _dot_nt(lhs, _block_diag(tile(bb_ref, i)))
        sk = _dot_nt(lhs, _block_diag(tile(kt_ref, i)))
        a_ab = jnp.where(strict, sb[:CHUNK], 0.0)
        tm_ref[i] = ident + a_ab
        np_ref[i] = a_ab.astype(BF16)
        arb_ref[i] = jnp.where(incl, sb[CHUNK:2 * CHUNK], 0.0).astype(BF16)
        bht_ref[i] = sb[2 * CHUNK:].astype(BF16)
        aak_ref[i, 0:CHUNK] = jnp.where(strict, sk[:CHUNK], 0.0).astype(BF16)
        aak_ref[i, CHUNK:2 * CHUNK] = jnp.where(incl, sk[CHUNK:2 * CHUNK], 0.0).astype(BF16)
        aak_ref[i, 2 * CHUNK:] = sk[2 * CHUNK:].astype(BF16)
    for i in range(ni):
        _, r0, c0 = inst[i]
        res = _dot(aak_ref[i], _block_diag(tile(v_ref, i)))
        av_ref[i] = res[:CHUNK].astype(BF16)
        y_ref[r0:r0 + CHUNK, c0:c0 + QUAD] = res[CHUNK:2 * CHUNK]
        nkv_ref[i] = res[2 * CHUNK:]
    for i in range(ni):
        n1 = np_ref[i]
        np_ref[i] = _dot(n1, _block_diag(n1)).astype(BF16)
    for _ in range(4):
        for i in range(ni):
            npow = np_ref[i]
            tm = tm_ref[i]
            res = _dot(jnp.concatenate([tm.astype(BF16), npow], axis=0), _block_diag(npow))
            tm_ref[i] = tm + res[:CHUNK]
            np_ref[i] = res[CHUNK:].astype(BF16)
    for i in range(ni):
        tm = tm_ref[i]
        tmb_ref[i] = (tm + _dot(tm.astype(BF16), _block_diag(np_ref[i]))).astype(BF16)
    for i in range(ni):
        rhs = jnp.concatenate([_block_diag(tile(aa_ref, i)), _block_diag(av_ref[i])], axis=1)
        w_ref[i] = _dot(tmb_ref[i], rhs).astype(BF16)
    for i in range(ni):
        c, r0, c0 = inst[i]
        w12 = w_ref[i]
        rhs = jnp.concatenate([_block_diag(w12[:, :QUAD]), _block_diag(w12[:, QUAD:])], axis=1)
        res = _dot(jnp.concatenate([arb_ref[i], bht_ref[i]], axis=0), rhs)
        pc = pc_ref[c:c + 1, c0:c0 + QUAD]
        qm_ref[i] = (tile(ra_ref, i).astype(F32) + res[:CHUNK, :QUAD]).astype(BF16)
        y_ref[r0:r0 + CHUNK, c0:c0 + QUAD] += res[:CHUNK, QUAD:]
        mq_ref[i] = (ident * pc + res[CHUNK:, :QUAD]).astype(BF16)
        nq_ref[i] = res[CHUNK:, QUAD:] + nkv_ref[i]
    for q in range(nq):
        s = jnp.where(first, s0_ref[0, q], sq_ref[q])
        for c in range(nchunk):
            i = c * nq + q
            _, r0, c0 = inst[i]
            res = _dot(jnp.concatenate([qm_ref[i], mq_ref[i]], axis=0), _block_diag(s.astype(BF16)))
            y_ref[r0:r0 + CHUNK, c0:c0 + QUAD] += res[:CHUNK]
            s = res[CHUNK:] + nq_ref[i]
        sq_ref[q] = s
        sout_ref[0, q] = s

    y = y_ref[...]
    mean = _head_sum(y, ones) * (1.0 / HEAD_DIM)
    d = y - mean
    var = _head_sum(d * d, ones) * (1.0 / HEAD_DIM)
    yn = d * lax.rsqrt(var + GN_EPS) * lg_ref[...] + lb_ref[...]
    out_ref[0] = (yn + bonus) * g


def _rwkv_mix(z, shift_prev, s0q, mu, w0, wa_bf, a0, g2_bf, k_k, k_a, r_k, lnx_g, lnx_b, tt):
    B, T, _ = z.shape
    nq = D_RWKV // QUAD
    ni = (tt // CHUNK) * nq
    idx = jnp.arange(QUAD) // HEAD_DIM
    ones = (idx[:, None] == idx[None, :]).astype(BF16)
    ti = jnp.arange(tt)
    tri = ((ti[:, None] // CHUNK == ti[None, :] // CHUNK) & (ti[None, :] <= ti[:, None])).astype(BF16)
    row = lambda b, t: (0, 0)
    vec = pl.BlockSpec((1, D_RWKV), row)
    return pl.pallas_call(
        functools.partial(_rwkv_kernel, tt=tt),
        grid=(B, T // tt),
        in_specs=[
            pl.BlockSpec((1, tt, D_SHIFT), lambda b, t: (b, t, 0)),
            pl.BlockSpec((1, 1, D_SHIFT), lambda b, t: (b, 0, 0)),
            pl.BlockSpec((1, nq, HEAD_DIM, QUAD), lambda b, t: (b, 0, 0, 0)),
            pl.BlockSpec((1, D_SHIFT), row),
            vec,
            pl.BlockSpec((LORA_W + LORA_A, 2 * D_RWKV), row),
            vec,
            pl.BlockSpec((LORA_G, D_RWKV), row),
            vec, vec, vec, vec, vec,
            pl.BlockSpec((QUAD, QUAD), row),
            pl.BlockSpec((tt, tt), row),
        ],
        out_specs=[
            pl.BlockSpec((1, tt, D_RWKV), lambda b, t: (b, t, 0)),
            pl.BlockSpec((1, 1, D_SHIFT), lambda b, t: (b, 0, 0)),
            pl.BlockSpec((1, nq, HEAD_DIM, QUAD), lambda b, t: (b, 0, 0, 0)),
        ],
        out_shape=[
            jax.ShapeDtypeStruct((B, T, D_RWKV), F32),
            jax.ShapeDtypeStruct((B, 1, D_SHIFT), F32),
            jax.ShapeDtypeStruct((B, nq, HEAD_DIM, QUAD), F32),
        ],
        scratch_shapes=[
            pltpu.VMEM((1, D_SHIFT), F32),
            pltpu.VMEM((nq, HEAD_DIM, QUAD), F32),
            pltpu.VMEM((tt, D_RWKV), BF16),
            pltpu.VMEM((tt, D_RWKV), BF16),
            pltpu.VMEM((tt, D_RWKV), BF16),
            pltpu.VMEM((tt, D_RWKV), BF16),
            pltpu.VMEM((tt, D_RWKV), BF16),
            pltpu.VMEM((tt // CHUNK, D_RWKV), F32),
            pltpu.VMEM((tt, D_RWKV), F32),
            pltpu.VMEM((ni, CHUNK, QUAD), F32),
            pltpu.VMEM((ni, CHUNK, QUAD), BF16),
            pltpu.VMEM((ni, CHUNK, QUAD), BF16),
            pltpu.VMEM((ni, 3 * CHUNK, QUAD), BF16),
            pltpu.VMEM((ni, CHUNK, QUAD), BF16),
            pltpu.VMEM((ni, CHUNK, QUAD), BF16),
            pltpu.VMEM((ni, CHUNK, QUAD), BF16),
            pltpu.VMEM((ni, CHUNK, QUAD), F32),
            pltpu.VMEM((ni, CHUNK, 2 * QUAD), BF16),
            pltpu.VMEM((ni, CHUNK, QUAD), BF16),
            pltpu.VMEM((ni, CHUNK, QUAD), BF16),
            pltpu.VMEM((ni, CHUNK, QUAD), F32),
        ],
        compiler_params=pltpu.CompilerParams(
            dimension_semantics=("arbitrary", "arbitrary"), vmem_limit_bytes=VMEM_LIMIT),
        name="rwkv_mix",
    )(z, shift_prev, s0q, mu, w0, wa_bf, a0, g2_bf, k_k, k_a, r_k, lnx_g, lnx_b, ones, tri)


def _out_ffn_kernel(x_ref, co_ref, ro_ref, mod_ref, gpost_ref, gfpre_ref, gfpost_ref,
                    wo_ref, wg_ref, wu_ref, wd_ref, y_ref):
    x = x_ref[0]
    m = mod_ref[0]
    mix = _dot(co_ref[0].astype(BF16), wo_ref[:D_CONV, :]) + _dot(ro_ref[0].astype(BF16), wo_ref[D_CONV:, :])
    x = x + (1.0 + m[2:3]) * _rms(mix, gpost_ref[...])
    h = (_rms(x, gfpre_ref[...]) * (1.0 + m[4:5]) + m[3:4]).astype(BF16)
    gate = _dot(h, wg_ref[...])
    up = _dot(h, wu_ref[...])
    act = (gate * _sigmoid(gate) * up).astype(BF16)
    f = _dot(act, wd_ref[...])
    y_ref[0] = x + (1.0 + m[5:6]) * _rms(f, gfpost_ref[...])


def _out_ffn(x, conv_out, rwkv_out, mod, g_post, g_fpre, g_fpost, wo, wg, wu, wd, tm):
    B, T, _ = x.shape
    d_ff = wg.shape[1]
    row = lambda b, t: (0, 0)
    once = dict(pipeline_mode=pl.Buffered(1))
    return pl.pallas_call(
        _out_ffn_kernel,
        grid=(B, T // tm),
        in_specs=[
            pl.BlockSpec((1, tm, D_MODEL), lambda b, t: (b, t, 0)),
            pl.BlockSpec((1, tm, D_CONV), lambda b, t: (b, t, 0)),
            pl.BlockSpec((1, tm, D_RWKV), lambda b, t: (b, t, 0)),
            pl.BlockSpec((1, 6, D_MODEL), lambda b, t: (b, 0, 0)),
            pl.BlockSpec((1, D_MODEL), row),
            pl.BlockSpec((1, D_MODEL), row),
            pl.BlockSpec((1, D_MODEL), row),
            pl.BlockSpec((D_MODEL, D_MODEL), row, **once),
            pl.BlockSpec((D_MODEL, d_ff), row, **once),
            pl.BlockSpec((D_MODEL, d_ff), row, **once),
            pl.BlockSpec((d_ff, D_MODEL), row, **once),
        ],
        out_specs=pl.BlockSpec((1, tm, D_MODEL), lambda b, t: (b, t, 0)),
        out_shape=jax.ShapeDtypeStruct((B, T, D_MODEL), F32),
        compiler_params=pltpu.CompilerParams(
            dimension_semantics=("arbitrary", "arbitrary"), vmem_limit_bytes=VMEM_LIMIT),
        name="out_ffn",
    )(x, conv_out, rwkv_out, mod, g_post, g_fpre, g_fpost, wo, wg, wu, wd)


def _state_to_quads(s):
    B = s.shape[0]
    s = s.reshape(B, N_HEADS // 4, 4, HEAD_DIM, HEAD_DIM)
    return s.transpose(0, 1, 4, 2, 3).reshape(B, N_HEADS // 4, HEAD_DIM, QUAD)


def _quads_to_state(sq):
    B = sq.shape[0]
    s = sq.reshape(B, N_HEADS // 4, HEAD_DIM, 4, HEAD_DIM)
    return s.transpose(0, 1, 3, 4, 2).reshape(B, N_HEADS, HEAD_DIM, HEAD_DIM)


def _layer(x, mod, conv_buf, shift_prev, wkv, p, tiles):
    tm, tc, tr = tiles
    glu, z = _inproj(x, mod, p["g_mix_pre"], p["w_in"], tm)
    conv_out, new_buf = _conv_mix(glu, conv_buf, p["conv_dw"], p["conv_b"], p["conv_ln_g"],
                                  p["conv_ln_b"], tc)
    rwkv_out, new_shift, sq = _rwkv_mix(
        z, shift_prev[:, None, :], _state_to_quads(wkv), p["mu_shift"], p["w0"], p["wa"], p["a0"],
        p["g2"], p["k_k"], p["k_a"], p["r_k"], p["lnx_g"], p["lnx_b"], tr)
    y = _out_ffn(x, conv_out, rwkv_out, mod, p["g_mix_post"], p["g_ffn_pre"], p["g_ffn_post"],
                 p["w_out"], p["w_gate"], p["w_up"], p["w_down"], tm)
    return y, new_buf, new_shift[:, 0, :], _quads_to_state(sq)


def _layer_params(l, w):
    row = lambda a: a[l].reshape(1, -1)
    wa = jnp.zeros((LORA_W + LORA_A, 2 * D_RWKV), F32)
    wa = wa.at[:LORA_W, :D_RWKV].set(w["w2"][l]).at[LORA_W:, D_RWKV:].set(w["a2"][l])
    return {
        "g_mix_pre": row(w["g_mix_pre"]), "g_mix_post": row(w["g_mix_post"]),
        "g_ffn_pre": row(w["g_ffn_pre"]), "g_ffn_post": row(w["g_ffn_post"]),
        "w_in": w["w_in"][l].astype(BF16),
        "conv_dw": w["conv_dw"][l], "conv_b": row(w["conv_b"]),
        "conv_ln_g": row(w["conv_ln_g"]), "conv_ln_b": row(w["conv_ln_b"]),
        "mu_shift": row(w["mu_shift"]), "w0": row(w["w0"]), "wa": wa.astype(BF16),
        "a0": row(w["a0"]), "g2": w["g2"][l].astype(BF16),
        "k_k": row(w["k_k"]), "k_a": row(w["k_a"]), "r_k": row(w["r_k"]),
        "lnx_g": row(w["lnx_g"]), "lnx_b": row(w["lnx_b"]),
        "w_out": w["w_out"][l].astype(BF16), "w_gate": w["w_gate"][l].astype(BF16),
        "w_up": w["w_up"][l].astype(BF16), "w_down": w["w_down"][l].astype(BF16),
    }


def _tiles(T):
    tm = min(T, 512)
    tc = min(T, 256)
    tr = min(T, 256)
    return tm, tc, tr


def kernel(x_prompt, x_sample, cache_conv, state_shift, state_wkv, c_prompt, c_sample, w_mod, b_mod, g_mix_pre, g_mix_post, g_ffn_pre, g_ffn_post, w_in, conv_dw, conv_b, conv_ln_g, conv_ln_b, mu_shift, w0, w2, a0, a2, g2, k_k, k_a, r_k, lnx_g, lnx_b, w_out, w_gate, w_up, w_down):
    w = dict(g_mix_pre=g_mix_pre, g_mix_post=g_mix_post, g_ffn_pre=g_ffn_pre, g_ffn_post=g_ffn_post,
             w_in=w_in, conv_dw=conv_dw, conv_b=conv_b, conv_ln_g=conv_ln_g, conv_ln_b=conv_ln_b,
             mu_shift=mu_shift, w0=w0, w2=w2, a0=a0, a2=a2, g2=g2, k_k=k_k, k_a=k_a, r_k=r_k,
             lnx_g=lnx_g, lnx_b=lnx_b, w_out=w_out, w_gate=w_gate, w_up=w_up, w_down=w_down)
    depth = w_mod.shape[0]
    bp, tp, _ = x_prompt.shape
    bs, ts, _ = x_sample.shape
    mod = _modulation(jnp.concatenate([c_prompt, c_sample], axis=0), w_mod, b_mod)
    mod = mod.reshape(depth, bp + bs, 6, D_MODEL)
    zero_conv = jnp.zeros((bp, CONV_BUF, D_CONV), F32)
    zero_shift = jnp.zeros((bp, D_SHIFT), F32)
    zero_wkv = jnp.zeros((bp, N_HEADS, HEAD_DIM, HEAD_DIM), F32)
    yp, ys = x_prompt, x_sample
    cp, sp, wp, cs, ss, wsm = [], [], [], [], [], []
    for l in range(depth):
        p = _layer_params(l, w)
        yp, b1, s1, k1 = _layer(yp, mod[l, :bp], zero_conv, zero_shift, zero_wkv, p, _tiles(tp))
        ys, b2, s2, k2 = _layer(ys, mod[l, bp:], cache_conv[l], state_shift[l], state_wkv[l], p,
                                _tiles(ts))
        cp.append(b1); sp.append(s1); wp.append(k1)
        cs.append(b2); ss.append(s2); wsm.append(k2)
    return (yp, ys, jnp.stack(cp), jnp.stack(sp), jnp.stack(wp),
            jnp.stack(cs), jnp.stack(ss), jnp.stack(wsm))
```

```python
import functools
import math

import jax
import jax.numpy as jnp
from jax import lax
from jax.experimental import pallas as pl
from jax.experimental.pallas import tpu as pltpu

F32 = jnp.float32
BF16 = jnp.bfloat16

D_MODEL = 1024
D_CONV = 512
D_RWKV = 512
HEAD_DIM = 64
N_HEADS = 8
CONV_WIDTH = 31
CONV_BUF = 30
LORA_W = 64
LORA_A = 64
LORA_G = 128
D_SHIFT = 3 * D_RWKV + LORA_W + LORA_A + LORA_G
D_IN = 2 * D_CONV + D_SHIFT
RMS_EPS = 1e-6
LN_EPS = 1e-5
GN_EPS = 64e-5
L2_EPS = 1e-12

CHUNK = 64
QUAD = 4 * HEAD_DIM
CONV_HALO = 32
VMEM_LIMIT = 56 * 1024 * 1024


def _dot(a, b):
    return jnp.dot(a, b, preferred_element_type=F32)


def _dot_nt(a, b):
    return lax.dot_general(a, b, (((1,), (1,)), ((), ())), preferred_element_type=F32)


def _dot_tn(a, b):
    return lax.dot_general(a, b, (((0,), (0,)), ((), ())), preferred_element_type=F32)


def _dot_hilo_lhs(m, x):
    hi = x.astype(BF16)
    lo = (x - hi.astype(F32)).astype(BF16)
    return _dot(m, hi) + _dot(m, lo)


def _head_sum(x, ones):
    xb = x.astype(BF16)
    return jnp.concatenate(
        [_dot(xb[:, q * QUAD:(q + 1) * QUAD], ones) for q in range(D_RWKV // QUAD)], axis=1)


def _sigmoid(x):
    return 1.0 / (1.0 + jnp.exp(-x))


def _rms(x, g):
    return x * lax.rsqrt(jnp.mean(x * x, axis=-1, keepdims=True) + RMS_EPS) * g


def _mod_kernel(c_ref, w_ref, b_ref, o_ref):
    c = c_ref[...]
    s = c * _sigmoid(c)
    o_ref[0] = jnp.dot(s, w_ref[0], preferred_element_type=F32,
                       precision=lax.Precision.HIGHEST) + b_ref[0]


def _modulation(c_all, w_mod, b_mod):
    L = w_mod.shape[0]
    bc = c_all.shape[0]
    n = w_mod.shape[2]
    tn = 1024
    return pl.pallas_call(
        _mod_kernel,
        grid=(L, n // tn),
        in_specs=[
            pl.BlockSpec((bc, D_MODEL), lambda l, j: (0, 0)),
            pl.BlockSpec((1, D_MODEL, tn), lambda l, j: (l, 0, j)),
            pl.BlockSpec((1, 1, tn), lambda l, j: (l, 0, j)),
        ],
        out_specs=pl.BlockSpec((1, bc, tn), lambda l, j: (l, 0, j)),
        out_shape=jax.ShapeDtypeStruct((L, bc, n), F32),
        compiler_params=pltpu.CompilerParams(
            dimension_semantics=("arbitrary", "arbitrary"), vmem_limit_bytes=VMEM_LIMIT),
        name="modulation",
    )(c_all, w_mod, b_mod.reshape(L, 1, n))


def _inproj_conv_kernel(x_ref, mod_ref, g_ref, w_ref, buf_ref, dw_ref, cb_ref, lg_ref, lb_ref,
                        z_ref, out_ref, nbuf_ref, win_ref, *, tt, rb):
    first = pl.program_id(1) == 0
    pad = CONV_HALO - CONV_BUF
    strip = D_CONV // 2

    x = x_ref[0]
    m = mod_ref[0]
    hb = (_rms(x, g_ref[...]) * (1.0 + m[1:2]) + m[0:1]).astype(BF16)
    u = _dot(hb, w_ref[:, :2 * D_CONV])
    win_ref[0:CONV_HALO, :] = jnp.where(first, buf_ref[0], win_ref[0:CONV_HALO, :])
    win_ref[CONV_HALO:CONV_HALO + tt, :] = u[:, :D_CONV] * _sigmoid(u[:, D_CONV:])
    z_ref[0] = _dot(hb, w_ref[:, 2 * D_CONV:])

    for r in range(tt // rb):
        base = r * rb
        accs = []
        for c0 in range(0, D_CONV, strip):
            cols = slice(c0, c0 + strip)
            acc = jnp.broadcast_to(cb_ref[:, cols], (rb, strip))
            for s in range(8):
                part = None
                for a in range((pad + CONV_WIDTH + 7) // 8):
                    j = 8 * a + s
                    if j < pad or j >= pad + CONV_WIDTH:
                        continue
                    lo = base + 8 * a
                    term = win_ref[lo:lo + rb + (8 if s else 0), cols] * dw_ref[j - pad:j - pad + 1, cols]
                    part = term if part is None else part + term
                acc = acc + part[s:s + rb]
            accs.append(acc)
        acc = jnp.concatenate(accs, axis=1)
        mu = jnp.mean(acc, axis=-1, keepdims=True)
        d = acc - mu
        var = jnp.mean(d * d, axis=-1, keepdims=True)
        y = d * lax.rsqrt(var + LN_EPS) * lg_ref[...] + lb_ref[...]
        out_ref[0, base:base + rb, :] = (y * _sigmoid(y)).astype(out_ref.dtype)

    tail = win_ref[tt:tt + CONV_HALO, :]
    win_ref[0:CONV_HALO, :] = tail
    nbuf_ref[0] = tail


def _inproj_conv(x, mod, g, w_in_bf, buf, dw, cb, lg, lb, tt):
    B, T, _ = x.shape
    rb = min(tt, 64)
    pad = CONV_HALO - CONV_BUF
    buf = jnp.pad(buf, ((0, 0), (pad, 0), (0, 0)))
    row = lambda b, t: (0, 0)
    z, out, nbuf = pl.pallas_call(
        functools.partial(_inproj_conv_kernel, tt=tt, rb=rb),
        grid=(B, T // tt),
        in_specs=[
            pl.BlockSpec((1, tt, D_MODEL), lambda b, t: (b, t, 0)),
            pl.BlockSpec((1, 6, D_MODEL), lambda b, t: (b, 0, 0)),
            pl.BlockSpec((1, D_MODEL), row),
            pl.BlockSpec((D_MODEL, D_IN), row),
            pl.BlockSpec((1, CONV_HALO, D_CONV), lambda b, t: (b, 0, 0)),
            pl.BlockSpec((CONV_WIDTH, D_CONV), row),
            pl.BlockSpec((1, D_CONV), row),
            pl.BlockSpec((1, D_CONV), row),
            pl.BlockSpec((1, D_CONV), row),
        ],
        out_specs=[
            pl.BlockSpec((1, tt, D_SHIFT), lambda b, t: (b, t, 0)),
            pl.BlockSpec((1, tt, D_CONV), lambda b, t: (b, t, 0)),
            pl.BlockSpec((1, CONV_HALO, D_CONV), lambda b, t: (b, 0, 0)),
        ],
        out_shape=[
            jax.ShapeDtypeStruct((B, T, D_SHIFT), F32),
            jax.ShapeDtypeStruct((B, T, D_CONV), BF16),
            jax.ShapeDtypeStruct((B, CONV_HALO, D_CONV), F32),
        ],
        scratch_shapes=[pltpu.VMEM((CONV_HALO + tt, D_CONV), F32)],
        compiler_params=pltpu.CompilerParams(
            dimension_semantics=("arbitrary", "arbitrary"), vmem_limit_bytes=VMEM_LIMIT),
        name="inproj_conv",
    )(x, mod, g, w_in_bf, buf, dw, cb, lg, lb)
    return z, out, nbuf[:, pad:, :]


def _block_diag(x):
    half = QUAD // 2
    lo = lax.broadcasted_iota(jnp.int32, (HEAD_DIM, half), 1) < HEAD_DIM
    zero = jnp.zeros((HEAD_DIM, half), x.dtype)
    xl, xr = x[:, :half], x[:, half:]
    rows = [
        jnp.concatenate([jnp.where(lo, xl, zero), zero], axis=1),
        jnp.concatenate([jnp.where(lo, zero, xl), zero], axis=1),
        jnp.concatenate([zero, jnp.where(lo, xr, zero)], axis=1),
        jnp.concatenate([zero, jnp.where(lo, zero, xr)], axis=1),
    ]
    return jnp.concatenate(rows, axis=0)


def _rwkv_kernel(z_ref, sprev_ref, s0_ref, mu_ref, w0_ref, wa_ref, a0_ref, g2_ref, kk_ref, ka_ref,
                 rk_ref, lg_ref, lb_ref, ones_ref, tri_ref,
                 out_ref, nshift_ref, sout_ref,
                 carry_ref, sq_ref, ra_ref, aa_ref, bb_ref, kt_ref, v_ref,
                 pc_ref, y_ref, tm_ref, np_ref, arb_ref, aak_ref, bht_ref, tmb_ref, av_ref,
                 rkv_ref, w_ref, qm_ref, mq_ref, nq_ref, yi_ref, bonus_ref, gate_ref,
                 *, tt, nt, ntiles):
    step = pl.program_id(0)
    first = jnp.minimum(step, ntiles - 1) % nt == 0
    first_prev = jnp.maximum(step - 1, 0) % nt == 0
    slot = step % 2
    nq = D_RWKV // QUAD
    nchunk = tt // CHUNK
    inst = [(c, c * CHUNK, q * QUAD) for c in range(nchunk) for q in range(nq)]
    ni = len(inst)

    z = z_ref[0]
    rows = lax.broadcasted_iota(jnp.int32, (tt, 1), 0)
    last_row = jnp.where(first, sprev_ref[0], carry_ref[...])
    z_prev = jnp.where(rows == 0, last_row, pltpu.roll(z, 1, 0))
    carry_ref[...] = z[tt - 1:tt, :]
    nshift_ref[0] = z[tt - 1:tt, :]
    zm = z + (z_prev - z) * mu_ref[...]
    r = zm[:, 0:D_RWKV]
    k = zm[:, D_RWKV:2 * D_RWKV]
    v = zm[:, 2 * D_RWKV:3 * D_RWKV]
    zwa = zm[:, 3 * D_RWKV:3 * D_RWKV + LORA_W + LORA_A]
    zg = zm[:, 3 * D_RWKV + LORA_W + LORA_A:]
    lane = lax.broadcasted_iota(jnp.int32, zwa.shape, 1)
    zwa = jnp.where(lane < LORA_W, jnp.tanh(zwa), zwa)
    wa = _dot(zwa.astype(BF16), wa_ref[...])
    lw = -math.exp(-0.5) * _sigmoid(w0_ref[...] + wa[:, :D_RWKV])
    a = _sigmoid(a0_ref[...] + wa[:, D_RWKV:])
    g = _dot(_sigmoid(zg).astype(BF16), g2_ref[...])

    ones = ones_ref[...]
    kk = k * kk_ref[...]
    kk = kk * lax.rsqrt(_head_sum(kk * kk, ones) + L2_EPS)
    k = k * (1.0 + (a - 1.0) * ka_ref[...])
    b = kk * a
    bonus = _head_sum(r * k * rk_ref[...], ones) * v

    cum = _dot_hilo_lhs(tri_ref[...], lw)
    e_neg = jnp.exp(-cum)
    ra_ref[...] = (r * jnp.exp(cum)).astype(BF16)
    aa_ref[...] = (-kk * jnp.exp(cum - lw)).astype(BF16)
    bb_ref[...] = (b * e_neg).astype(BF16)
    kt_ref[...] = (k * e_neg).astype(BF16)
    v_ref[...] = v.astype(BF16)
    for c in range(nchunk):
        pc_ref[c:c + 1, :] = jnp.exp(cum[(c + 1) * CHUNK - 1:(c + 1) * CHUNK, :])
    bonus_ref[slot] = bonus
    gate_ref[slot] = g

    state = [jnp.where(first_prev, s0_ref[0, q], sq_ref[q]) for q in range(nq)]

    def state_step(i):
        c, r0, c0 = inst[i]
        q = i % nq
        res = _dot(jnp.concatenate([qm_ref[i], mq_ref[i]], axis=0),
                   _block_diag(state[q].astype(BF16)))
        y_ref[r0:r0 + CHUNK, c0:c0 + QUAD] = yi_ref[i] + res[:CHUNK]
        state[q] = res[CHUNK:] + nq_ref[i]
        if c == nchunk - 1:
            sq_ref[q] = state[q]
            sout_ref[0, q] = state[q]

    def finish_prev():
        y = y_ref[...]
        mean = _head_sum(y, ones) * (1.0 / HEAD_DIM)
        d = y - mean
        var = _head_sum(d * d, ones) * (1.0 / HEAD_DIM)
        yn = d * lax.rsqrt(var + GN_EPS) * lg_ref[...] + lb_ref[...]
        out_ref[0] = ((yn + bonus_ref[1 - slot]) * gate_ref[1 - slot]).astype(out_ref.dtype)

    pending = [functools.partial(state_step, i) for i in range(ni)] + [finish_prev]
    n_iter = 9 * ni
    emitted = [0]

    def weave():
        emitted[0] += 1
        while pending and (ni + 1 - len(pending)) * n_iter < emitted[0] * (ni + 1):
            pending.pop(0)()

    rowc = lax.broadcasted_iota(jnp.int32, (CHUNK, QUAD), 0)
    colc = lax.broadcasted_iota(jnp.int32, (CHUNK, QUAD), 1) % CHUNK
    strict = colc < rowc
    incl = colc <= rowc
    ident = jnp.where(colc == rowc, 1.0, 0.0)

    def tile(ref, i):
        _, r0, c0 = inst[i]
        return ref[r0:r0 + CHUNK, c0:c0 + QUAD]

    for i in range(ni):
        c, r0, c0 = inst[i]
        pc = pc_ref[c:c + 1, c0:c0 + QUAD]
        lhs = jnp.concatenate([tile(aa_ref, i), tile(ra_ref, i), (ident * pc).astype(BF16)], axis=0)
        sb = _dot_nt(lhs, _block_diag(tile(bb_ref, i)))
        sk = _dot_nt(lhs, _block_diag(tile(kt_ref, i)))
        a_ab = jnp.where(strict, sb[:CHUNK], 0.0)
        tm_ref[i] = ident + a_ab
        np_ref[i] = a_ab.astype(BF16)
        arb_ref[i] = jnp.where(incl, sb[CHUNK:2 * CHUNK], 0.0).astype(BF16)
        bht_ref[i] = sb[2 * CHUNK:].astype(BF16)
        aak_ref[i, 0:CHUNK] = jnp.where(strict, sk[:CHUNK], 0.0).astype(BF16)
        aak_ref[i, CHUNK:2 * CHUNK] = jnp.where(incl, sk[CHUNK:2 * CHUNK], 0.0).astype(BF16)
        aak_ref[i, 2 * CHUNK:] = sk[2 * CHUNK:].astype(BF16)
        weave()
    for i in range(ni):
        res = _dot(aak_ref[i], _block_diag(tile(v_ref, i)))
        av_ref[i] = res[:CHUNK].astype(BF16)
        rkv_ref[i] = res[CHUNK:]
        weave()
    for i in range(ni):
        n1 = np_ref[i]
        np_ref[i] = _dot(n1, _block_diag(n1)).astype(BF16)
        weave()
    for _ in range(4):
        for i in range(ni):
            npow = np_ref[i]
            tm = tm_ref[i]
            res = _dot(jnp.concatenate([tm.astype(BF16), npow], axis=0), _block_diag(npow))
            tm_ref[i] = tm + res[:CHUNK]
            np_ref[i] = res[CHUNK:].astype(BF16)
            weave()
    for i in range(ni):
        tm = tm_ref[i]
        tmb_ref[i] = (tm + _dot(tm.astype(BF16), _block_diag(np_ref[i]))).astype(BF16)
        weave()
    for i in range(ni):
        rhs = jnp.concatenate([_block_diag(tile(aa_ref, i)), _block_diag(av_ref[i])], axis=1)
        w_ref[i] = _dot(tmb_ref[i], rhs).astype(BF16)
        weave()
    while pending:
        pending.pop(0)()
    for i in range(ni):
        c, _, c0 = inst[i]
        w12 = w_ref[i]
        rhs = jnp.concatenate([_block_diag(w12[:, :QUAD]), _block_diag(w12[:, QUAD:])], axis=1)
        res = _dot(jnp.concatenate([arb_ref[i], bht_ref[i]], axis=0), rhs)
        pc = pc_ref[c:c + 1, c0:c0 + QUAD]
        qm_ref[i] = (tile(ra_ref, i).astype(F32) + res[:CHUNK, :QUAD]).astype(BF16)
        yi_ref[i] = res[:CHUNK, QUAD:] + rkv_ref[i, :CHUNK]
        mq_ref[i] = (ident * pc + res[CHUNK:, :QUAD]).astype(BF16)
        nq_ref[i] = res[CHUNK:, QUAD:] + rkv_ref[i, CHUNK:]


def _rwkv_mix(z, shift_prev, s0q, mu, w0, wa_bf, a0, g2_bf, k_k, k_a, r_k, lnx_g, lnx_b, tt):
    B, T, _ = z.shape
    nq = D_RWKV // QUAD
    ni = (tt // CHUNK) * nq
    idx = jnp.arange(QUAD) // HEAD_DIM
    ones = (idx[:, None] == idx[None, :]).astype(BF16)
    ti = jnp.arange(tt)
    tri = ((ti[:, None] // CHUNK == ti[None, :] // CHUNK) & (ti[None, :] <= ti[:, None])).astype(BF16)
    nt = T // tt
    ntiles = B * nt
    cur = lambda s: jnp.minimum(s, ntiles - 1)
    prev = lambda s: jnp.maximum(s - 1, 0)
    row = lambda s: (0, 0)
    vec = pl.BlockSpec((1, D_RWKV), row)
    return pl.pallas_call(
        functools.partial(_rwkv_kernel, tt=tt, nt=nt, ntiles=ntiles),
        grid=(ntiles + 1,),
        in_specs=[
            pl.BlockSpec((1, tt, D_SHIFT), lambda s: (cur(s) // nt, cur(s) % nt, 0)),
            pl.BlockSpec((1, 1, D_SHIFT), lambda s: (cur(s) // nt, 0, 0)),
            pl.BlockSpec((1, nq, HEAD_DIM, QUAD), lambda s: (prev(s) // nt, 0, 0, 0)),
            pl.BlockSpec((1, D_SHIFT), row),
            vec,
            pl.BlockSpec((LORA_W + LORA_A, 2 * D_RWKV), row),
            vec,
            pl.BlockSpec((LORA_G, D_RWKV), row),
            vec, vec, vec, vec, vec,
            pl.BlockSpec((QUAD, QUAD), row),
            pl.BlockSpec((tt, tt), row),
        ],
        out_specs=[
            pl.BlockSpec((1, tt, D_RWKV), lambda s: (prev(s) // nt, prev(s) % nt, 0)),
            pl.BlockSpec((1, 1, D_SHIFT), lambda s: (cur(s) // nt, 0, 0)),
            pl.BlockSpec((1, nq, HEAD_DIM, QUAD), lambda s: (prev(s) // nt, 0, 0, 0)),
        ],
        out_shape=[
            jax.ShapeDtypeStruct((B, T, D_RWKV), BF16),
            jax.ShapeDtypeStruct((B, 1, D_SHIFT), F32),
            jax.ShapeDtypeStruct((B, nq, HEAD_DIM, QUAD), F32),
        ],
        scratch_shapes=[
            pltpu.VMEM((1, D_SHIFT), F32),
            pltpu.VMEM((nq, HEAD_DIM, QUAD), F32),
            pltpu.VMEM((tt, D_RWKV), BF16),
            pltpu.VMEM((tt, D_RWKV), BF16),
            pltpu.VMEM((tt, D_RWKV), BF16),
            pltpu.VMEM((tt, D_RWKV), BF16),
            pltpu.VMEM((tt, D_RWKV), BF16),
            pltpu.VMEM((tt // CHUNK, D_RWKV), F32),
            pltpu.VMEM((tt, D_RWKV), F32),
            pltpu.VMEM((ni, CHUNK, QUAD), F32),
            pltpu.VMEM((ni, CHUNK, QUAD), BF16),
            pltpu.VMEM((ni, CHUNK, QUAD), BF16),
            pltpu.VMEM((ni, 3 * CHUNK, QUAD), BF16),
            pltpu.VMEM((ni, CHUNK, QUAD), BF16),
            pltpu.VMEM((ni, CHUNK, QUAD), BF16),
            pltpu.VMEM((ni, CHUNK, QUAD), BF16),
            pltpu.VMEM((ni, 2 * CHUNK, QUAD), F32),
            pltpu.VMEM((ni, CHUNK, 2 * QUAD), BF16),
            pltpu.VMEM((ni, CHUNK, QUAD), BF16),
            pltpu.VMEM((ni, CHUNK, QUAD), BF16),
            pltpu.VMEM((ni, CHUNK, QUAD), F32),
            pltpu.VMEM((ni, CHUNK, QUAD), F32),
            pltpu.VMEM((2, tt, D_RWKV), F32),
            pltpu.VMEM((2, tt, D_RWKV), F32),
        ],
        compiler_params=pltpu.CompilerParams(
            dimension_semantics=("arbitrary",), vmem_limit_bytes=VMEM_LIMIT),
        name="rwkv_mix",
    )(z, shift_prev, s0q, mu, w0, wa_bf, a0, g2_bf, k_k, k_a, r_k, lnx_g, lnx_b, ones, tri)


def _out_ffn_kernel(x_ref, co_ref, ro_ref, mod_ref, gpost_ref, gfpre_ref, gfpost_ref,
                    wo_ref, wg_ref, wu_ref, wd_ref, y_ref):
    x = x_ref[0]
    m = mod_ref[0]
    mix = _dot(co_ref[0], wo_ref[:D_CONV, :]) + _dot(ro_ref[0], wo_ref[D_CONV:, :])
    x = x + (1.0 + m[2:3]) * _rms(mix, gpost_ref[...])
    h = (_rms(x, gfpre_ref[...]) * (1.0 + m[4:5]) + m[3:4]).astype(BF16)
    gate = _dot(h, wg_ref[...])
    up = _dot(h, wu_ref[...])
    act = (gate * _sigmoid(gate) * up).astype(BF16)
    f = _dot(act, wd_ref[...])
    y_ref[0] = x + (1.0 + m[5:6]) * _rms(f, gfpost_ref[...])


def _out_ffn(x, conv_out, rwkv_out, mod, g_post, g_fpre, g_fpost, wo, wg, wu, wd, tm):
    B, T, _ = x.shape
    d_ff = wg.shape[1]
    row = lambda b, t: (0, 0)
    once = dict(pipeline_mode=pl.Buffered(1))
    return pl.pallas_call(
        _out_ffn_kernel,
        grid=(B, T // tm),
        in_specs=[
            pl.BlockSpec((1, tm, D_MODEL), lambda b, t: (b, t, 0)),
            pl.BlockSpec((1, tm, D_CONV), lambda b, t: (b, t, 0)),
            pl.BlockSpec((1, tm, D_RWKV), lambda b, t: (b, t, 0)),
            pl.BlockSpec((1, 6, D_MODEL), lambda b, t: (b, 0, 0)),
            pl.BlockSpec((1, D_MODEL), row),
            pl.BlockSpec((1, D_MODEL), row),
            pl.BlockSpec((1, D_MODEL), row),
            pl.BlockSpec((D_MODEL, D_MODEL), row, **once),
            pl.BlockSpec((D_MODEL, d_ff), row, **once),
            pl.BlockSpec((D_MODEL, d_ff), row, **once),
            pl.BlockSpec((d_ff, D_MODEL), row, **once),
        ],
        out_specs=pl.BlockSpec((1, tm, D_MODEL), lambda b, t: (b, t, 0)),
        out_shape=jax.ShapeDtypeStruct((B, T, D_MODEL), F32),
        compiler_params=pltpu.CompilerParams(
            dimension_semantics=("arbitrary", "arbitrary"), vmem_limit_bytes=VMEM_LIMIT),
        name="out_ffn",
    )(x, conv_out, rwkv_out, mod, g_post, g_fpre, g_fpost, wo, wg, wu, wd)


def _state_to_quads(s):
    B = s.shape[0]
    s = s.reshape(B, N_HEADS // 4, 4, HEAD_DIM, HEAD_DIM)
    return s.transpose(0, 1, 4, 2, 3).reshape(B, N_HEADS // 4, HEAD_DIM, QUAD)


def _quads_to_state(sq):
    B = sq.shape[0]
    s = sq.reshape(B, N_HEADS // 4, HEAD_DIM, 4, HEAD_DIM)
    return s.transpose(0, 1, 3, 4, 2).reshape(B, N_HEADS, HEAD_DIM, HEAD_DIM)


def _layer(x, mod, conv_buf, shift_prev, wkv, p, tiles):
    tm, tr = tiles
    z, conv_out, new_buf = _inproj_conv(x, mod, p["g_mix_pre"], p["w_in"], conv_buf, p["conv_dw"],
                                        p["conv_b"], p["conv_ln_g"], p["conv_ln_b"], tm)
    rwkv_out, new_shift, sq = _rwkv_mix(
        z, shift_prev[:, None, :], _state_to_quads(wkv), p["mu_shift"], p["w0"], p["wa"], p["a0"],
        p["g2"], p["k_k"], p["k_a"], p["r_k"], p["lnx_g"], p["lnx_b"], tr)
    y = _out_ffn(x, conv_out, rwkv_out, mod, p["g_mix_post"], p["g_ffn_pre"], p["g_ffn_post"],
                 p["w_out"], p["w_gate"], p["w_up"], p["w_down"], tm)
    return y, new_buf, new_shift[:, 0, :], _quads_to_state(sq)


def _layer_params(l, w):
    row = lambda a: a[l].reshape(1, -1)
    wa = jnp.zeros((LORA_W + LORA_A, 2 * D_RWKV), F32)
    wa = wa.at[:LORA_W, :D_RWKV].set(w["w2"][l]).at[LORA_W:, D_RWKV:].set(w["a2"][l])
    return {
        "g_mix_pre": row(w["g_mix_pre"]), "g_mix_post": row(w["g_mix_post"]),
        "g_ffn_pre": row(w["g_ffn_pre"]), "g_ffn_post": row(w["g_ffn_post"]),
        "w_in": w["w_in"][l].astype(BF16),
        "conv_dw": w["conv_dw"][l], "conv_b": row(w["conv_b"]),
        "conv_ln_g": row(w["conv_ln_g"]), "conv_ln_b": row(w["conv_ln_b"]),
        "mu_shift": row(w["mu_shift"]), "w0": row(w["w0"]), "wa": wa.astype(BF16),
        "a0": row(w["a0"]), "g2": w["g2"][l].astype(BF16),
        "k_k": row(w["k_k"]), "k_a": row(w["k_a"]), "r_k": row(w["r_k"]),
        "lnx_g": row(w["lnx_g"]), "lnx_b": row(w["lnx_b"]),
        "w_out": w["w_out"][l].astype(BF16), "w_gate": w["w_gate"][l].astype(BF16),
        "w_up": w["w_up"][l].astype(BF16), "w_down": w["w_down"][l].astype(BF16),
    }


def _tiles(T):
    tm = min(T, 512)
    tr = min(T, 512)
    return tm, tr


def kernel(x_prompt, x_sample, cache_conv, state_shift, state_wkv, c_prompt, c_sample, w_mod, b_mod, g_mix_pre, g_mix_post, g_ffn_pre, g_ffn_post, w_in, conv_dw, conv_b, conv_ln_g, conv_ln_b, mu_shift, w0, w2, a0, a2, g2, k_k, k_a, r_k, lnx_g, lnx_b, w_out, w_gate, w_up, w_down):
    w = dict(g_mix_pre=g_mix_pre, g_mix_post=g_mix_post, g_ffn_pre=g_ffn_pre, g_ffn_post=g_ffn_post,
             w_in=w_in, conv_dw=conv_dw, conv_b=conv_b, conv_ln_g=conv_ln_g, conv_ln_b=conv_ln_b,
             mu_shift=mu_shift, w0=w0, w2=w2, a0=a0, a2=a2, g2=g2, k_k=k_k, k_a=k_a, r_k=r_k,
             lnx_g=lnx_g, lnx_b=lnx_b, w_out=w_out, w_gate=w_gate, w_up=w_up, w_down=w_down)
    depth = w_mod.shape[0]
    bp, tp, _ = x_prompt.shape
    bs, ts, _ = x_sample.shape
    mod = _modulation(jnp.concatenate([c_prompt, c_sample], axis=0), w_mod, b_mod)
    mod = mod.reshape(depth, bp + bs, 6, D_MODEL)
    zero_conv = jnp.zeros((bp, CONV_BUF, D_CONV), F32)
    zero_shift = jnp.zeros((bp, D_SHIFT), F32)
    zero_wkv = jnp.zeros((bp, N_HEADS, HEAD_DIM, HEAD_DIM), F32)
    yp, ys = x_prompt, x_sample
    cp, sp, wp, cs, ss, wsm = [], [], [], [], [], []
    for l in range(depth):
        p = _layer_params(l, w)
        yp, b1, s1, k1 = _layer(yp, mod[l, :bp], zero_conv, zero_shift, zero_wkv, p, _tiles(tp))
        ys, b2, s2, k2 = _layer(ys, mod[l, bp:], cache_conv[l], state_shift[l], state_wkv[l], p,
                                _tiles(ts))
        cp.append(b1); sp.append(s1); wp.append(k1)
        cs.append(b2); ss.append(s2); wsm.append(k2)
    return (yp, ys, jnp.stack(cp), jnp.stack(sp), jnp.stack(wp),
            jnp.stack(cs), jnp.stack(ss), jnp.stack(wsm))
```

```python
import functools
import math

import jax
import jax.numpy as jnp
from jax import lax
from jax.experimental import pallas as pl
from jax.experimental.pallas import tpu as pltpu

F32 = jnp.float32
BF16 = jnp.bfloat16

D_MODEL = 1024
D_CONV = 512
D_RWKV = 512
HEAD_DIM = 64
N_HEADS = 8
CONV_WIDTH = 31
CONV_BUF = 30
LORA_W = 64
LORA_A = 64
LORA_G = 128
D_SHIFT = 3 * D_RWKV + LORA_W + LORA_A + LORA_G
D_IN = 2 * D_CONV + D_SHIFT
RMS_EPS = 1e-6
LN_EPS = 1e-5
GN_EPS = 64e-5
L2_EPS = 1e-12

CHUNK = 64
QUAD = 4 * HEAD_DIM
CONV_HALO = 32
VMEM_LIMIT = 56 * 1024 * 1024


def _dot(a, b):
    return jnp.dot(a, b, preferred_element_type=F32)


def _dot_nt(a, b):
    return lax.dot_general(a, b, (((1,), (1,)), ((), ())), preferred_element_type=F32)


def _dot_tn(a, b):
    return lax.dot_general(a, b, (((0,), (0,)), ((), ())), preferred_element_type=F32)


def _dot_hilo_lhs(m, x):
    hi = x.astype(BF16)
    lo = (x - hi.astype(F32)).astype(BF16)
    return _dot(m, hi) + _dot(m, lo)


def _head_sum(x, ones):
    xb = x.astype(BF16)
    return jnp.concatenate(
        [_dot(xb[:, q * QUAD:(q + 1) * QUAD], ones) for q in range(D_RWKV // QUAD)], axis=1)


def _sigmoid(x):
    return 1.0 / (1.0 + jnp.exp(-x))


def _rms(x, g):
    return x * lax.rsqrt(jnp.mean(x * x, axis=-1, keepdims=True) + RMS_EPS) * g


def _mod_kernel(c_ref, w_ref, b_ref, o_ref):
    c = c_ref[...]
    s = c * _sigmoid(c)
    o_ref[0] = jnp.dot(s, w_ref[0], preferred_element_type=F32,
                       precision=lax.Precision.HIGHEST) + b_ref[0]


def _modulation(c_all, w_mod, b_mod):
    L = w_mod.shape[0]
    bc = c_all.shape[0]
    n = w_mod.shape[2]
    tn = 1024
    return pl.pallas_call(
        _mod_kernel,
        grid=(L, n // tn),
        in_specs=[
            pl.BlockSpec((bc, D_MODEL), lambda l, j: (0, 0)),
            pl.BlockSpec((1, D_MODEL, tn), lambda l, j: (l, 0, j)),
            pl.BlockSpec((1, 1, tn), lambda l, j: (l, 0, j)),
        ],
        out_specs=pl.BlockSpec((1, bc, tn), lambda l, j: (l, 0, j)),
        out_shape=jax.ShapeDtypeStruct((L, bc, n), F32),
        compiler_params=pltpu.CompilerParams(
            dimension_semantics=("arbitrary", "arbitrary"), vmem_limit_bytes=VMEM_LIMIT),
        name="modulation",
    )(c_all, w_mod, b_mod.reshape(L, 1, n))


def _inproj_conv_kernel(x_ref, mod_ref, g_ref, w_ref, buf_ref, dw_ref, cb_ref, lg_ref, lb_ref,
                        z_ref, out_ref, nbuf_ref, win_ref, *, tt, rb):
    first = pl.program_id(1) == 0
    pad = CONV_HALO - CONV_BUF
    strip = 128

    m = mod_ref[0]
    win_ref[0:CONV_HALO, :] = jnp.where(first, buf_ref[0], win_ref[0:CONV_HALO, :])

    for r in range(tt // rb):
        base = r * rb
        x = x_ref[0, base:base + rb, :]
        hb = (_rms(x, g_ref[...]) * (1.0 + m[1:2]) + m[0:1]).astype(BF16)
        u = _dot(hb, w_ref[:, :2 * D_CONV])
        win_ref[CONV_HALO + base:CONV_HALO + base + rb, :] = u[:, :D_CONV] * _sigmoid(u[:, D_CONV:])
        z_ref[0, base:base + rb, :] = _dot(hb, w_ref[:, 2 * D_CONV:])
        accs = []
        for c0 in range(0, D_CONV, strip):
            cols = slice(c0, c0 + strip)
            acc = jnp.broadcast_to(cb_ref[:, cols], (rb, strip))
            for s in range(8):
                part = None
                for a in range((pad + CONV_WIDTH + 7) // 8):
                    j = 8 * a + s
                    if j < pad or j >= pad + CONV_WIDTH:
                        continue
                    lo = base + 8 * a
                    term = win_ref[lo:lo + rb + (8 if s else 0), cols] * dw_ref[j - pad:j - pad + 1, cols]
                    part = term if part is None else part + term
                acc = acc + part[s:s + rb]
            accs.append(acc)
        acc = jnp.concatenate(accs, axis=1)
        mu = jnp.mean(acc, axis=-1, keepdims=True)
        d = acc - mu
        var = jnp.mean(d * d, axis=-1, keepdims=True)
        y = d * lax.rsqrt(var + LN_EPS) * lg_ref[...] + lb_ref[...]
        out_ref[0, base:base + rb, :] = (y * _sigmoid(y)).astype(out_ref.dtype)

    tail = win_ref[tt:tt + CONV_HALO, :]
    win_ref[0:CONV_HALO, :] = tail
    nbuf_ref[0] = tail


def _inproj_conv(x, mod, g, w_in_bf, buf, dw, cb, lg, lb, tt):
    B, T, _ = x.shape
    rb = min(tt, 128)
    pad = CONV_HALO - CONV_BUF
    buf = jnp.pad(buf, ((0, 0), (pad, 0), (0, 0)))
    row = lambda b, t: (0, 0)
    z, out, nbuf = pl.pallas_call(
        functools.partial(_inproj_conv_kernel, tt=tt, rb=rb),
        grid=(B, T // tt),
        in_specs=[
            pl.BlockSpec((1, tt, D_MODEL), lambda b, t: (b, t, 0)),
            pl.BlockSpec((1, 6, D_MODEL), lambda b, t: (b, 0, 0)),
            pl.BlockSpec((1, D_MODEL), row),
            pl.BlockSpec((D_MODEL, D_IN), row),
            pl.BlockSpec((1, CONV_HALO, D_CONV), lambda b, t: (b, 0, 0)),
            pl.BlockSpec((CONV_WIDTH, D_CONV), row),
            pl.BlockSpec((1, D_CONV), row),
            pl.BlockSpec((1, D_CONV), row),
            pl.BlockSpec((1, D_CONV), row),
        ],
        out_specs=[
            pl.BlockSpec((1, tt, D_SHIFT), lambda b, t: (b, t, 0)),
            pl.BlockSpec((1, tt, D_CONV), lambda b, t: (b, t, 0)),
            pl.BlockSpec((1, CONV_HALO, D_CONV), lambda b, t: (b, 0, 0)),
        ],
        out_shape=[
            jax.ShapeDtypeStruct((B, T, D_SHIFT), F32),
            jax.ShapeDtypeStruct((B, T, D_CONV), BF16),
            jax.ShapeDtypeStruct((B, CONV_HALO, D_CONV), F32),
        ],
        scratch_shapes=[pltpu.VMEM((CONV_HALO + tt, D_CONV), F32)],
        compiler_params=pltpu.CompilerParams(
            dimension_semantics=("arbitrary", "arbitrary"), vmem_limit_bytes=VMEM_LIMIT),
        name="inproj_conv",
    )(x, mod, g, w_in_bf, buf, dw, cb, lg, lb)
    return z, out, nbuf[:, pad:, :]


def _block_diag(x):
    half = QUAD // 2
    lo = lax.broadcasted_iota(jnp.int32, (HEAD_DIM, half), 1) < HEAD_DIM
    zero = jnp.zeros((HEAD_DIM, half), x.dtype)
    xl, xr = x[:, :half], x[:, half:]
    rows = [
        jnp.concatenate([jnp.where(lo, xl, zero), zero], axis=1),
        jnp.concatenate([jnp.where(lo, zero, xl), zero], axis=1),
        jnp.concatenate([zero, jnp.where(lo, xr, zero)], axis=1),
        jnp.concatenate([zero, jnp.where(lo, zero, xr)], axis=1),
    ]
    return jnp.concatenate(rows, axis=0)


def _rwkv_kernel(z_ref, sprev_ref, s0_ref, mu_ref, w0_ref, wa_ref, a0_ref, g2_ref, kk_ref, ka_ref,
                 rk_ref, lg_ref, lb_ref, ones_ref, tri_ref,
                 out_ref, nshift_ref, sout_ref,
                 carry_ref, sq_ref, ra_ref, aa_ref, bb_ref, kt_ref, v_ref,
                 pc_ref, y_ref, tm_ref, np_ref, arb_ref, aak_ref, bht_ref, tmb_ref, av_ref,
                 rkv_ref, w_ref, qm_ref, mq_ref, nq_ref, yi_ref, bonus_ref, gate_ref,
                 *, tt, nt, ntiles, pb):
    step = pl.program_id(0)
    first = jnp.minimum(step, ntiles - 1) % nt == 0
    first_prev = jnp.maximum(step - 2, 0) % nt == 0
    wslot = step % 2
    rslot = 1 - wslot
    bwslot = step % 3
    brslot = (step + 1) % 3
    nq = D_RWKV // QUAD
    nchunk = tt // CHUNK
    inst = [(c, c * CHUNK, q * QUAD) for c in range(nchunk) for q in range(nq)]
    ni = len(inst)
    ones = ones_ref[...]

    def prep(r0):
        z = z_ref[0, r0:r0 + pb, :]
        if r0 == 0:
            last_row = jnp.where(first, sprev_ref[0], carry_ref[...])
        else:
            last_row = z_ref[0, r0 - 1:r0, :]
        rows = lax.broadcasted_iota(jnp.int32, (pb, 1), 0)
        z_prev = jnp.where(rows == 0, last_row, pltpu.roll(z, 1, 0))
        if r0 + pb == tt:
            carry_ref[...] = z[pb - 1:pb, :]
            nshift_ref[0] = z[pb - 1:pb, :]
        zm = z + (z_prev - z) * mu_ref[...]
        r = zm[:, 0:D_RWKV]
        k = zm[:, D_RWKV:2 * D_RWKV]
        v = zm[:, 2 * D_RWKV:3 * D_RWKV]
        zwa = zm[:, 3 * D_RWKV:3 * D_RWKV + LORA_W + LORA_A]
        zg = zm[:, 3 * D_RWKV + LORA_W + LORA_A:]
        lane = lax.broadcasted_iota(jnp.int32, zwa.shape, 1)
        zwa = jnp.where(lane < LORA_W, jnp.tanh(zwa), zwa)
        wa = _dot(zwa.astype(BF16), wa_ref[...])
        lw = -math.exp(-0.5) * _sigmoid(w0_ref[...] + wa[:, :D_RWKV])
        a = _sigmoid(a0_ref[...] + wa[:, D_RWKV:])
        g = _dot(_sigmoid(zg).astype(BF16), g2_ref[...])
        kk = k * kk_ref[...]
        kk = kk * lax.rsqrt(_head_sum(kk * kk, ones) + L2_EPS)
        k = k * (1.0 + (a - 1.0) * ka_ref[...])
        b = kk * a
        bonus = _head_sum(r * k * rk_ref[...], ones) * v
        cum = _dot_hilo_lhs(tri_ref[...], lw)
        e_neg = jnp.exp(-cum)
        rs = slice(r0, r0 + pb)
        ra_ref[wslot, rs, :] = (r * jnp.exp(cum)).astype(BF16)
        aa_ref[wslot, rs, :] = (-kk * jnp.exp(cum - lw)).astype(BF16)
        bb_ref[wslot, rs, :] = (b * e_neg).astype(BF16)
        kt_ref[wslot, rs, :] = (k * e_neg).astype(BF16)
        v_ref[wslot, rs, :] = v.astype(BF16)
        for c in range(pb // CHUNK):
            cc = r0 // CHUNK + c
            pc_ref[wslot, cc:cc + 1, :] = jnp.exp(cum[(c + 1) * CHUNK - 1:(c + 1) * CHUNK, :])
        bonus_ref[bwslot, rs, :] = bonus
        gate_ref[bwslot, rs, :] = g

    state = [jnp.where(first_prev, s0_ref[0, q], sq_ref[q]) for q in range(nq)]

    def state_step(i):
        c, r0, c0 = inst[i]
        q = i % nq
        res = _dot(jnp.concatenate([qm_ref[i], mq_ref[i]], axis=0),
                   _block_diag(state[q].astype(BF16)))
        y_ref[r0:r0 + CHUNK, c0:c0 + QUAD] = yi_ref[i] + res[:CHUNK]
        state[q] = res[CHUNK:] + nq_ref[i]
        if c == nchunk - 1:
            sq_ref[q] = state[q]
            sout_ref[0, q] = state[q]

    def finish_prev():
        y = y_ref[...]
        mean = _head_sum(y, ones) * (1.0 / HEAD_DIM)
        d = y - mean
        var = _head_sum(d * d, ones) * (1.0 / HEAD_DIM)
        yn = d * lax.rsqrt(var + GN_EPS) * lg_ref[...] + lb_ref[...]
        out_ref[0] = ((yn + bonus_ref[brslot]) * gate_ref[brslot]).astype(out_ref.dtype)

    pending = [functools.partial(state_step, i) for i in range(ni)] + [finish_prev]
    preps = [functools.partial(prep, r0) for r0 in range(0, tt, pb)]
    n_iter = 9 * ni
    sq_lo, sq_hi = 2 * ni, 8 * ni
    emitted = [0]

    def weave():
        emitted[0] += 1
        while pending and (ni + 1 - len(pending)) * n_iter < emitted[0] * (ni + 1):
            pending.pop(0)()
        done = tt // pb - len(preps)
        while preps and emitted[0] > sq_lo and done * (sq_hi - sq_lo) < (emitted[0] - sq_lo) * (tt // pb):
            preps.pop(0)()
            done += 1

    rowc = lax.broadcasted_iota(jnp.int32, (CHUNK, QUAD), 0)
    colc = lax.broadcasted_iota(jnp.int32, (CHUNK, QUAD), 1) % CHUNK
    strict = colc < rowc
    incl = colc <= rowc
    ident = jnp.where(colc == rowc, 1.0, 0.0)

    def tile(ref, i):
        _, r0, c0 = inst[i]
        return ref[rslot, r0:r0 + CHUNK, c0:c0 + QUAD]

    for i in range(ni):
        c, r0, c0 = inst[i]
        pc = pc_ref[rslot, c:c + 1, c0:c0 + QUAD]
        lhs = jnp.concatenate([tile(aa_ref, i), tile(ra_ref, i), (ident * pc).astype(BF16)], axis=0)
        sb = _dot_nt(lhs, _block_diag(tile(bb_ref, i)))
        sk = _dot_nt(lhs, _block_diag(tile(kt_ref, i)))
        a_ab = jnp.where(strict, sb[:CHUNK], 0.0)
        tm_ref[i] = ident + a_ab
        np_ref[i] = a_ab.astype(BF16)
        arb_ref[i] = jnp.where(incl, sb[CHUNK:2 * CHUNK], 0.0).astype(BF16)
        bht_ref[i] = sb[2 * CHUNK:].astype(BF16)
        aak_ref[i, 0:CHUNK] = jnp.where(strict, sk[:CHUNK], 0.0).astype(BF16)
        aak_ref[i, CHUNK:2 * CHUNK] = jnp.where(incl, sk[CHUNK:2 * CHUNK], 0.0).astype(BF16)
        aak_ref[i, 2 * CHUNK:] = sk[2 * CHUNK:].astype(BF16)
        weave()
    for i in range(ni):
        res = _dot(aak_ref[i], _block_diag(tile(v_ref, i)))
        av_ref[i] = res[:CHUNK].astype(BF16)
        rkv_ref[i] = res[CHUNK:]
        weave()
    for i in range(ni):
        n1 = np_ref[i]
        np_ref[i] = _dot(n1, _block_diag(n1)).astype(BF16)
        weave()
    for _ in range(4):
        for i in range(ni):
            npow = np_ref[i]
            tm = tm_ref[i]
            res = _dot(jnp.concatenate([tm.astype(BF16), npow], axis=0), _block_diag(npow))
            tm_ref[i] = tm + res[:CHUNK]
            np_ref[i] = res[CHUNK:].astype(BF16)
            weave()
    for i in range(ni):
        tm = tm_ref[i]
        tmb_ref[i] = (tm + _dot(tm.astype(BF16), _block_diag(np_ref[i]))).astype(BF16)
        weave()
    for i in range(ni):
        rhs = jnp.concatenate([_block_diag(tile(aa_ref, i)), _block_diag(av_ref[i])], axis=1)
        w_ref[i] = _dot(tmb_ref[i], rhs).astype(BF16)
        weave()
    while pending:
        pending.pop(0)()
    while preps:
        preps.pop(0)()
    for i in range(ni):
        c, _, c0 = inst[i]
        w12 = w_ref[i]
        rhs = jnp.concatenate([_block_diag(w12[:, :QUAD]), _block_diag(w12[:, QUAD:])], axis=1)
        res = _dot(jnp.concatenate([arb_ref[i], bht_ref[i]], axis=0), rhs)
        pc = pc_ref[rslot, c:c + 1, c0:c0 + QUAD]
        qm_ref[i] = (tile(ra_ref, i).astype(F32) + res[:CHUNK, :QUAD]).astype(BF16)
        yi_ref[i] = res[:CHUNK, QUAD:] + rkv_ref[i, :CHUNK]
        mq_ref[i] = (ident * pc + res[CHUNK:, :QUAD]).astype(BF16)
        nq_ref[i] = res[CHUNK:, QUAD:] + rkv_ref[i, CHUNK:]


def _rwkv_mix(z, shift_prev, s0q, mu, w0, wa_bf, a0, g2_bf, k_k, k_a, r_k, lnx_g, lnx_b, tt):
    B, T, _ = z.shape
    nq = D_RWKV // QUAD
    ni = (tt // CHUNK) * nq
    idx = jnp.arange(QUAD) // HEAD_DIM
    ones = (idx[:, None] == idx[None, :]).astype(BF16)
    pb = min(tt, 64)
    ti = jnp.arange(pb)
    tri = ((ti[:, None] // CHUNK == ti[None, :] // CHUNK) & (ti[None, :] <= ti[:, None])).astype(BF16)
    nt = T // tt
    ntiles = B * nt
    cur = lambda s: jnp.minimum(s, ntiles - 1)
    prev = lambda s: jnp.maximum(s - 2, 0)
    row = lambda s: (0, 0)
    vec = pl.BlockSpec((1, D_RWKV), row)
    return pl.pallas_call(
        functools.partial(_rwkv_kernel, tt=tt, nt=nt, ntiles=ntiles, pb=pb),
        grid=(ntiles + 2,),
        in_specs=[
            pl.BlockSpec((1, tt, D_SHIFT), lambda s: (cur(s) // nt, cur(s) % nt, 0)),
            pl.BlockSpec((1, 1, D_SHIFT), lambda s: (cur(s) // nt, 0, 0)),
            pl.BlockSpec((1, nq, HEAD_DIM, QUAD), lambda s: (prev(s) // nt, 0, 0, 0)),
            pl.BlockSpec((1, D_SHIFT), row),
            vec,
            pl.BlockSpec((LORA_W + LORA_A, 2 * D_RWKV), row),
            vec,
            pl.BlockSpec((LORA_G, D_RWKV), row),
            vec, vec, vec, vec, vec,
            pl.BlockSpec((QUAD, QUAD), row),
            pl.BlockSpec((pb, pb), row),
        ],
        out_specs=[
            pl.BlockSpec((1, tt, D_RWKV), lambda s: (prev(s) // nt, prev(s) % nt, 0)),
            pl.BlockSpec((1, 1, D_SHIFT), lambda s: (cur(s) // nt, 0, 0)),
            pl.BlockSpec((1, nq, HEAD_DIM, QUAD), lambda s: (prev(s) // nt, 0, 0, 0)),
        ],
        out_shape=[
            jax.ShapeDtypeStruct((B, T, D_RWKV), BF16),
            jax.ShapeDtypeStruct((B, 1, D_SHIFT), F32),
            jax.ShapeDtypeStruct((B, nq, HEAD_DIM, QUAD), F32),
        ],
        scratch_shapes=[
            pltpu.VMEM((1, D_SHIFT), F32),
            pltpu.VMEM((nq, HEAD_DIM, QUAD), F32),
            pltpu.VMEM((2, tt, D_RWKV), BF16),
            pltpu.VMEM((2, tt, D_RWKV), BF16),
            pltpu.VMEM((2, tt, D_RWKV), BF16),
            pltpu.VMEM((2, tt, D_RWKV), BF16),
            pltpu.VMEM((2, tt, D_RWKV), BF16),
            pltpu.VMEM((2, tt // CHUNK, D_RWKV), F32),
            pltpu.VMEM((tt, D_RWKV), F32),
            pltpu.VMEM((ni, CHUNK, QUAD), F32),
            pltpu.VMEM((ni, CHUNK, QUAD), BF16),
            pltpu.VMEM((ni, CHUNK, QUAD), BF16),
            pltpu.VMEM((ni, 3 * CHUNK, QUAD), BF16),
            pltpu.VMEM((ni, CHUNK, QUAD), BF16),
            pltpu.VMEM((ni, CHUNK, QUAD), BF16),
            pltpu.VMEM((ni, CHUNK, QUAD), BF16),
            pltpu.VMEM((ni, 2 * CHUNK, QUAD), F32),
            pltpu.VMEM((ni, CHUNK, 2 * QUAD), BF16),
            pltpu.VMEM((ni, CHUNK, QUAD), BF16),
            pltpu.VMEM((ni, CHUNK, QUAD), BF16),
            pltpu.VMEM((ni, CHUNK, QUAD), F32),
            pltpu.VMEM((ni, CHUNK, QUAD), F32),
            pltpu.VMEM((3, tt, D_RWKV), F32),
            pltpu.VMEM((3, tt, D_RWKV), F32),
        ],
        compiler_params=pltpu.CompilerParams(
            dimension_semantics=("arbitrary",), vmem_limit_bytes=VMEM_LIMIT),
        name="rwkv_mix",
    )(z, shift_prev, s0q, mu, w0, wa_bf, a0, g2_bf, k_k, k_a, r_k, lnx_g, lnx_b, ones, tri)


def _out_ffn_kernel(x_ref, co_ref, ro_ref, mod_ref, gpost_ref, gfpre_ref, gfpost_ref,
                    wo_ref, wg_ref, wu_ref, wd_ref, y_ref):
    m = mod_ref[0]
    tm = x_ref.shape[1]
    nsplit = 2 if tm % 32 == 0 else 1
    rows = [slice(i * tm // nsplit, (i + 1) * tm // nsplit) for i in range(nsplit)]

    def mix(r):
        return _dot(co_ref[0, r, :], wo_ref[:D_CONV, :]) + _dot(ro_ref[0, r, :], wo_ref[D_CONV:, :])

    def resid(r, mx):
        return x_ref[0, r, :] + (1.0 + m[2:3]) * _rms(mx, gpost_ref[...])

    def pre(x1):
        return (_rms(x1, gfpre_ref[...]) * (1.0 + m[4:5]) + m[3:4]).astype(BF16)

    def hidden(h):
        gate = _dot(h, wg_ref[...])
        up = _dot(h, wu_ref[...])
        return (gate * _sigmoid(gate) * up).astype(BF16)

    mixes = [mix(r) for r in rows]
    x1 = [None] * nsplit
    act = [None] * nsplit
    for i, r in enumerate(rows):
        x1[i] = resid(r, mixes[i])
        act[i] = hidden(pre(x1[i]))
    for i, r in enumerate(rows):
        f = _dot(act[i], wd_ref[...])
        y_ref[0, r, :] = x1[i] + (1.0 + m[5:6]) * _rms(f, gfpost_ref[...])


def _out_ffn(x, conv_out, rwkv_out, mod, g_post, g_fpre, g_fpost, wo, wg, wu, wd, tm):
    B, T, _ = x.shape
    d_ff = wg.shape[1]
    row = lambda b, t: (0, 0)
    once = dict(pipeline_mode=pl.Buffered(1))
    return pl.pallas_call(
        _out_ffn_kernel,
        grid=(B, T // tm),
        in_specs=[
            pl.BlockSpec((1, tm, D_MODEL), lambda b, t: (b, t, 0)),
            pl.BlockSpec((1, tm, D_CONV), lambda b, t: (b, t, 0)),
            pl.BlockSpec((1, tm, D_RWKV), lambda b, t: (b, t, 0)),
            pl.BlockSpec((1, 6, D_MODEL), lambda b, t: (b, 0, 0)),
            pl.BlockSpec((1, D_MODEL), row),
            pl.BlockSpec((1, D_MODEL), row),
            pl.BlockSpec((1, D_MODEL), row),
            pl.BlockSpec((D_MODEL, D_MODEL), row, **once),
            pl.BlockSpec((D_MODEL, d_ff), row, **once),
            pl.BlockSpec((D_MODEL, d_ff), row, **once),
            pl.BlockSpec((d_ff, D_MODEL), row, **once),
        ],
        out_specs=pl.BlockSpec((1, tm, D_MODEL), lambda b, t: (b, t, 0)),
        out_shape=jax.ShapeDtypeStruct((B, T, D_MODEL), F32),
        compiler_params=pltpu.CompilerParams(
            dimension_semantics=("arbitrary", "arbitrary"), vmem_limit_bytes=VMEM_LIMIT),
        name="out_ffn",
    )(x, conv_out, rwkv_out, mod, g_post, g_fpre, g_fpost, wo, wg, wu, wd)


def _state_to_quads(s):
    B = s.shape[0]
    s = s.reshape(B, N_HEADS // 4, 4, HEAD_DIM, HEAD_DIM)
    return s.transpose(0, 1, 4, 2, 3).reshape(B, N_HEADS // 4, HEAD_DIM, QUAD)


def _quads_to_state(sq):
    B = sq.shape[0]
    s = sq.reshape(B, N_HEADS // 4, HEAD_DIM, 4, HEAD_DIM)
    return s.transpose(0, 1, 3, 4, 2).reshape(B, N_HEADS, HEAD_DIM, HEAD_DIM)


def _layer(x, mod, conv_buf, shift_prev, wkv, p, tiles):
    tm, tr = tiles
    z, conv_out, new_buf = _inproj_conv(x, mod, p["g_mix_pre"], p["w_in"], conv_buf, p["conv_dw"],
                                        p["conv_b"], p["conv_ln_g"], p["conv_ln_b"], tm)
    rwkv_out, new_shift, sq = _rwkv_mix(
        z, shift_prev[:, None, :], _state_to_quads(wkv), p["mu_shift"], p["w0"], p["wa"], p["a0"],
        p["g2"], p["k_k"], p["k_a"], p["r_k"], p["lnx_g"], p["lnx_b"], tr)
    y = _out_ffn(x, conv_out, rwkv_out, mod, p["g_mix_post"], p["g_ffn_pre"], p["g_ffn_post"],
                 p["w_out"], p["w_gate"], p["w_up"], p["w_down"], tm)
    return y, new_buf, new_shift[:, 0, :], _quads_to_state(sq)


def _layer_params(l, w):
    row = lambda a: a[l].reshape(1, -1)
    wa = jnp.zeros((LORA_W + LORA_A, 2 * D_RWKV), F32)
    wa = wa.at[:LORA_W, :D_RWKV].set(w["w2"][l]).at[LORA_W:, D_RWKV:].set(w["a2"][l])
    return {
        "g_mix_pre": row(w["g_mix_pre"]), "g_mix_post": row(w["g_mix_post"]),
        "g_ffn_pre": row(w["g_ffn_pre"]), "g_ffn_post": row(w["g_ffn_post"]),
        "w_in": w["w_in"][l].astype(BF16),
        "conv_dw": w["conv_dw"][l], "conv_b": row(w["conv_b"]),
        "conv_ln_g": row(w["conv_ln_g"]), "conv_ln_b": row(w["conv_ln_b"]),
        "mu_shift": row(w["mu_shift"]), "w0": row(w["w0"]), "wa": wa.astype(BF16),
        "a0": row(w["a0"]), "g2": w["g2"][l].astype(BF16),
        "k_k": row(w["k_k"]), "k_a": row(w["k_a"]), "r_k": row(w["r_k"]),
        "lnx_g": row(w["lnx_g"]), "lnx_b": row(w["lnx_b"]),
        "w_out": w["w_out"][l].astype(BF16), "w_gate": w["w_gate"][l].astype(BF16),
        "w_up": w["w_up"][l].astype(BF16), "w_down": w["w_down"][l].astype(BF16),
    }


def _tiles(T):
    tm = min(T, 512)
    tr = min(T, 512)
    return tm, tr


def kernel(x_prompt, x_sample, cache_conv, state_shift, state_wkv, c_prompt, c_sample, w_mod, b_mod, g_mix_pre, g_mix_post, g_ffn_pre, g_ffn_post, w_in, conv_dw, conv_b, conv_ln_g, conv_ln_b, mu_shift, w0, w2, a0, a2, g2, k_k, k_a, r_k, lnx_g, lnx_b, w_out, w_gate, w_up, w_down):
    w = dict(g_mix_pre=g_mix_pre, g_mix_post=g_mix_post, g_ffn_pre=g_ffn_pre, g_ffn_post=g_ffn_post,
             w_in=w_in, conv_dw=conv_dw, conv_b=conv_b, conv_ln_g=conv_ln_g, conv_ln_b=conv_ln_b,
             mu_shift=mu_shift, w0=w0, w2=w2, a0=a0, a2=a2, g2=g2, k_k=k_k, k_a=k_a, r_k=r_k,
             lnx_g=lnx_g, lnx_b=lnx_b, w_out=w_out, w_gate=w_gate, w_up=w_up, w_down=w_down)
    depth = w_mod.shape[0]
    bp, tp, _ = x_prompt.shape
    bs, ts, _ = x_sample.shape
    mod = _modulation(jnp.concatenate([c_prompt, c_sample], axis=0), w_mod, b_mod)
    mod = mod.reshape(depth, bp + bs, 6, D_MODEL)
    zero_conv = jnp.zeros((bp, CONV_BUF, D_CONV), F32)
    zero_shift = jnp.zeros((bp, D_SHIFT), F32)
    zero_wkv = jnp.zeros((bp, N_HEADS, HEAD_DIM, HEAD_DIM), F32)
    yp, ys = x_prompt, x_sample
    cp, sp, wp, cs, ss, wsm = [], [], [], [], [], []
    for l in range(depth):
        p = _layer_params(l, w)
        yp, b1, s1, k1 = _layer(yp, mod[l, :bp], zero_conv, zero_shift, zero_wkv, p, _tiles(tp))
        ys, b2, s2, k2 = _layer(ys, mod[l, bp:], cache_conv[l], state_shift[l], state_wkv[l], p,
                                _tiles(ts))
        cp.append(b1); sp.append(s1); wp.append(k1)
        cs.append(b2); ss.append(s2); wsm.append(k2)
    return (yp, ys, jnp.stack(cp), jnp.stack(sp), jnp.stack(wp),
            jnp.stack(cs), jnp.stack(ss), jnp.stack(wsm))
```

```python
import functools
import math

import jax
import jax.numpy as jnp
from jax import lax
from jax.experimental import pallas as pl
from jax.experimental.pallas import tpu as pltpu

F32 = jnp.float32
BF16 = jnp.bfloat16

D_MODEL = 1024
D_CONV = 512
D_RWKV = 512
HEAD_DIM = 64
N_HEADS = 8
CONV_WIDTH = 31
CONV_BUF = 30
LORA_W = 64
LORA_A = 64
LORA_G = 128
D_SHIFT = 3 * D_RWKV + LORA_W + LORA_A + LORA_G
D_IN = 2 * D_CONV + D_SHIFT
RMS_EPS = 1e-6
LN_EPS = 1e-5
GN_EPS = 64e-5
L2_EPS = 1e-12

CHUNK = 64
QUAD = 4 * HEAD_DIM
CONV_HALO = 32
VMEM_LIMIT = 56 * 1024 * 1024


def _dot(a, b):
    return jnp.dot(a, b, preferred_element_type=F32)


def _dot_nt(a, b):
    return lax.dot_general(a, b, (((1,), (1,)), ((), ())), preferred_element_type=F32)


def _dot_tn(a, b):
    return lax.dot_general(a, b, (((0,), (0,)), ((), ())), preferred_element_type=F32)


def _dot_hilo_lhs(m, x):
    hi = x.astype(BF16)
    lo = (x - hi.astype(F32)).astype(BF16)
    return _dot(m, hi) + _dot(m, lo)


def _head_sum(x, ones):
    xb = x.astype(BF16)
    return jnp.concatenate(
        [_dot(xb[:, q * QUAD:(q + 1) * QUAD], ones) for q in range(D_RWKV // QUAD)], axis=1)


def _sigmoid(x):
    return 1.0 / (1.0 + jnp.exp(-x))


def _rms(x, g):
    return x * lax.rsqrt(jnp.mean(x * x, axis=-1, keepdims=True) + RMS_EPS) * g


def _mod_kernel(c_ref, w_ref, b_ref, o_ref):
    c = c_ref[...]
    s = c * _sigmoid(c)
    o_ref[0] = jnp.dot(s, w_ref[0], preferred_element_type=F32,
                       precision=lax.Precision.HIGHEST) + b_ref[0]


def _modulation(c_all, w_mod, b_mod):
    L = w_mod.shape[0]
    bc = c_all.shape[0]
    n = w_mod.shape[2]
    tn = 1024
    return pl.pallas_call(
        _mod_kernel,
        grid=(L, n // tn),
        in_specs=[
            pl.BlockSpec((bc, D_MODEL), lambda l, j: (0, 0)),
            pl.BlockSpec((1, D_MODEL, tn), lambda l, j: (l, 0, j)),
            pl.BlockSpec((1, 1, tn), lambda l, j: (l, 0, j)),
        ],
        out_specs=pl.BlockSpec((1, bc, tn), lambda l, j: (l, 0, j)),
        out_shape=jax.ShapeDtypeStruct((L, bc, n), F32),
        compiler_params=pltpu.CompilerParams(
            dimension_semantics=("arbitrary", "arbitrary"), vmem_limit_bytes=VMEM_LIMIT),
        name="modulation",
    )(c_all, w_mod, b_mod.reshape(L, 1, n))


def _inproj_conv_kernel(x_ref, mod_ref, g_ref, w_ref, buf_ref, dw_ref, cb_ref, lg_ref, lb_ref,
                        z_ref, out_ref, nbuf_ref, win_ref, *, tt, rb):
    first = pl.program_id(1) == 0
    pad = CONV_HALO - CONV_BUF
    strip = D_CONV // 2

    m = mod_ref[0]
    hb = (_rms(x_ref[0], g_ref[...]) * (1.0 + m[1:2]) + m[0:1]).astype(BF16)
    u = _dot(hb, w_ref[:, :2 * D_CONV])
    win_ref[0:CONV_HALO, :] = jnp.where(first, buf_ref[0], win_ref[0:CONV_HALO, :])
    win_ref[CONV_HALO:CONV_HALO + tt, :] = u[:, :D_CONV] * _sigmoid(u[:, D_CONV:])
    z_ref[0] = _dot(hb, w_ref[:, 2 * D_CONV:])

    for r in range(tt // rb):
        base = r * rb
        accs = []
        for c0 in range(0, D_CONV, strip):
            cols = slice(c0, c0 + strip)
            acc = jnp.broadcast_to(cb_ref[:, cols], (rb, strip))
            for s in range(8):
                part = None
                for a in range((pad + CONV_WIDTH + 7) // 8):
                    j = 8 * a + s
                    if j < pad or j >= pad + CONV_WIDTH:
                        continue
                    lo = base + 8 * a
                    term = win_ref[lo:lo + rb + (8 if s else 0), cols] * dw_ref[j - pad:j - pad + 1, cols]
                    part = term if part is None else part + term
                acc = acc + part[s:s + rb]
            accs.append(acc)
        acc = jnp.concatenate(accs, axis=1)
        mu = jnp.mean(acc, axis=-1, keepdims=True)
        d = acc - mu
        var = jnp.mean(d * d, axis=-1, keepdims=True)
        y = d * lax.rsqrt(var + LN_EPS) * lg_ref[...] + lb_ref[...]
        out_ref[0, base:base + rb, :] = (y * _sigmoid(y)).astype(out_ref.dtype)

    tail = win_ref[tt:tt + CONV_HALO, :]
    win_ref[0:CONV_HALO, :] = tail
    nbuf_ref[0] = tail


def _inproj_conv(x, mod, g, w_in_bf, buf, dw, cb, lg, lb, tt):
    B, T, _ = x.shape
    rb = min(tt, 64)
    pad = CONV_HALO - CONV_BUF
    buf = jnp.pad(buf, ((0, 0), (pad, 0), (0, 0)))
    row = lambda b, t: (0, 0)
    z, out, nbuf = pl.pallas_call(
        functools.partial(_inproj_conv_kernel, tt=tt, rb=rb),
        grid=(B, T // tt),
        in_specs=[
            pl.BlockSpec((1, tt, D_MODEL), lambda b, t: (b, t, 0)),
            pl.BlockSpec((1, 6, D_MODEL), lambda b, t: (b, 0, 0)),
            pl.BlockSpec((1, D_MODEL), row),
            pl.BlockSpec((D_MODEL, D_IN), row),
            pl.BlockSpec((1, CONV_HALO, D_CONV), lambda b, t: (b, 0, 0)),
            pl.BlockSpec((CONV_WIDTH, D_CONV), row),
            pl.BlockSpec((1, D_CONV), row),
            pl.BlockSpec((1, D_CONV), row),
            pl.BlockSpec((1, D_CONV), row),
        ],
        out_specs=[
            pl.BlockSpec((1, tt, D_SHIFT), lambda b, t: (b, t, 0)),
            pl.BlockSpec((1, tt, D_CONV), lambda b, t: (b, t, 0)),
            pl.BlockSpec((1, CONV_HALO, D_CONV), lambda b, t: (b, 0, 0)),
        ],
        out_shape=[
            jax.ShapeDtypeStruct((B, T, D_SHIFT), F32),
            jax.ShapeDtypeStruct((B, T, D_CONV), BF16),
            jax.ShapeDtypeStruct((B, CONV_HALO, D_CONV), F32),
        ],
        scratch_shapes=[pltpu.VMEM((CONV_HALO + tt, D_CONV), F32)],
        compiler_params=pltpu.CompilerParams(
            dimension_semantics=("arbitrary", "arbitrary"), vmem_limit_bytes=VMEM_LIMIT),
        name="inproj_conv",
    )(x, mod, g, w_in_bf, buf, dw, cb, lg, lb)
    return z, out, nbuf[:, pad:, :]


def _block_diag(x):
    half = QUAD // 2
    lo = lax.broadcasted_iota(jnp.int32, (HEAD_DIM, half), 1) < HEAD_DIM
    zero = jnp.zeros((HEAD_DIM, half), x.dtype)
    xl, xr = x[:, :half], x[:, half:]
    rows = [
        jnp.concatenate([jnp.where(lo, xl, zero), zero], axis=1),
        jnp.concatenate([jnp.where(lo, zero, xl), zero], axis=1),
        jnp.concatenate([zero, jnp.where(lo, xr, zero)], axis=1),
        jnp.concatenate([zero, jnp.where(lo, zero, xr)], axis=1),
    ]
    return jnp.concatenate(rows, axis=0)


def _rwkv_kernel(z_ref, sprev_ref, s0_ref, mu_ref, w0_ref, wa_ref, a0_ref, g2_ref, kk_ref, ka_ref,
                 rk_ref, lg_ref, lb_ref, ones_ref, tri_ref,
                 out_ref, nshift_ref, sout_ref,
                 carry_ref, sq_ref, ra_ref, aa_ref, bb_ref, kt_ref, v_ref,
                 pc_ref, y_ref, tm_ref, np_ref, arb_ref, aak_ref, bht_ref, tmb_ref, av_ref,
                 rkv_ref, w_ref, qm_ref, mq_ref, nq_ref, yi_ref, bonus_ref, gate_ref,
                 *, tt, nt, ntiles, pb):
    step = pl.program_id(0)
    first = jnp.minimum(step, ntiles - 1) % nt == 0
    first_prev = jnp.maximum(step - 2, 0) % nt == 0
    wslot = step % 2
    rslot = 1 - wslot
    bwslot = step % 3
    brslot = (step + 1) % 3
    nq = D_RWKV // QUAD
    nchunk = tt // CHUNK
    inst = [(c, c * CHUNK, q * QUAD) for c in range(nchunk) for q in range(nq)]
    ni = len(inst)
    ones = ones_ref[...]

    def prep(r0):
        z = z_ref[0, r0:r0 + pb, :]
        if r0 == 0:
            last_row = jnp.where(first, sprev_ref[0], carry_ref[...])
        else:
            last_row = z_ref[0, r0 - 1:r0, :]
        rows = lax.broadcasted_iota(jnp.int32, (pb, 1), 0)
        z_prev = jnp.where(rows == 0, last_row, pltpu.roll(z, 1, 0))
        if r0 + pb == tt:
            carry_ref[...] = z[pb - 1:pb, :]
            nshift_ref[0] = z[pb - 1:pb, :]
        zm = z + (z_prev - z) * mu_ref[...]
        r = zm[:, 0:D_RWKV]
        k = zm[:, D_RWKV:2 * D_RWKV]
        v = zm[:, 2 * D_RWKV:3 * D_RWKV]
        zwa = zm[:, 3 * D_RWKV:3 * D_RWKV + LORA_W + LORA_A]
        zg = zm[:, 3 * D_RWKV + LORA_W + LORA_A:]
        lane = lax.broadcasted_iota(jnp.int32, zwa.shape, 1)
        zwa = jnp.where(lane < LORA_W, jnp.tanh(zwa), zwa)
        wa = _dot(zwa.astype(BF16), wa_ref[...])
        lw = -math.exp(-0.5) * _sigmoid(w0_ref[...] + wa[:, :D_RWKV])
        a = _sigmoid(a0_ref[...] + wa[:, D_RWKV:])
        g = _dot(_sigmoid(zg).astype(BF16), g2_ref[...])
        kk = k * kk_ref[...]
        k = k * (1.0 + (a - 1.0) * ka_ref[...])
        sums = _head_sum(jnp.concatenate([kk * kk, r * k * rk_ref[...]], axis=0), ones)
        kk = kk * lax.rsqrt(sums[:pb] + L2_EPS)
        b = kk * a
        bonus = sums[pb:] * v
        cum = _dot_hilo_lhs(tri_ref[...], lw)
        e_neg = jnp.exp(-cum)
        rs = slice(r0, r0 + pb)
        ra_ref[wslot, rs, :] = (r * jnp.exp(cum)).astype(BF16)
        aa_ref[wslot, rs, :] = (-kk * jnp.exp(cum - lw)).astype(BF16)
        bb_ref[wslot, rs, :] = (b * e_neg).astype(BF16)
        kt_ref[wslot, rs, :] = (k * e_neg).astype(BF16)
        v_ref[wslot, rs, :] = v.astype(BF16)
        for c in range(pb // CHUNK):
            cc = r0 // CHUNK + c
            pc_ref[wslot, cc:cc + 1, :] = jnp.exp(cum[(c + 1) * CHUNK - 1:(c + 1) * CHUNK, :])
        bonus_ref[bwslot, rs, :] = bonus
        gate_ref[bwslot, rs, :] = g

    state = [jnp.where(first_prev, s0_ref[0, q], sq_ref[q]) for q in range(nq)]

    def state_step(i):
        c, r0, c0 = inst[i]
        q = i % nq
        res = _dot(jnp.concatenate([qm_ref[i], mq_ref[i]], axis=0),
                   _block_diag(state[q].astype(BF16)))
        y_ref[r0:r0 + CHUNK, c0:c0 + QUAD] = yi_ref[i] + res[:CHUNK]
        state[q] = res[CHUNK:] + nq_ref[i]
        if c == nchunk - 1:
            sq_ref[q] = state[q]
            sout_ref[0, q] = state[q]

    def finish_prev():
        y = y_ref[...]
        mean = _head_sum(y, ones) * (1.0 / HEAD_DIM)
        d = y - mean
        var = _head_sum(d * d, ones) * (1.0 / HEAD_DIM)
        yn = d * lax.rsqrt(var + GN_EPS) * lg_ref[...] + lb_ref[...]
        out_ref[0] = ((yn + bonus_ref[brslot]) * gate_ref[brslot]).astype(out_ref.dtype)

    pending = [functools.partial(state_step, i) for i in range(ni)] + [finish_prev]
    preps = [functools.partial(prep, r0) for r0 in range(0, tt, pb)]
    n_iter = 9 * ni
    sq_lo, sq_hi = 2 * ni, 8 * ni
    emitted = [0]

    def weave():
        emitted[0] += 1
        while pending and (ni + 1 - len(pending)) * n_iter < emitted[0] * (ni + 1):
            pending.pop(0)()
        done = tt // pb - len(preps)
        while preps and emitted[0] > sq_lo and done * (sq_hi - sq_lo) < (emitted[0] - sq_lo) * (tt // pb):
            preps.pop(0)()
            done += 1

    rowc = lax.broadcasted_iota(jnp.int32, (CHUNK, QUAD), 0)
    colc = lax.broadcasted_iota(jnp.int32, (CHUNK, QUAD), 1) % CHUNK
    strict = colc < rowc
    incl = colc <= rowc
    ident = jnp.where(colc == rowc, 1.0, 0.0)

    def tile(ref, i):
        _, r0, c0 = inst[i]
        return ref[rslot, r0:r0 + CHUNK, c0:c0 + QUAD]

    for i in range(ni):
        c, r0, c0 = inst[i]
        pc = pc_ref[rslot, c:c + 1, c0:c0 + QUAD]
        lhs = jnp.concatenate([tile(aa_ref, i), tile(ra_ref, i), (ident * pc).astype(BF16)], axis=0)
        sb = _dot_nt(lhs, _block_diag(tile(bb_ref, i)))
        sk = _dot_nt(lhs, _block_diag(tile(kt_ref, i)))
        a_ab = jnp.where(strict, sb[:CHUNK], 0.0)
        tm_ref[i] = ident + a_ab
        np_ref[i] = a_ab.astype(BF16)
        arb_ref[i] = jnp.where(incl, sb[CHUNK:2 * CHUNK], 0.0).astype(BF16)
        bht_ref[i] = sb[2 * CHUNK:].astype(BF16)
        aak_ref[i, 0:CHUNK] = jnp.where(strict, sk[:CHUNK], 0.0).astype(BF16)
        aak_ref[i, CHUNK:2 * CHUNK] = jnp.where(incl, sk[CHUNK:2 * CHUNK], 0.0).astype(BF16)
        aak_ref[i, 2 * CHUNK:] = sk[2 * CHUNK:].astype(BF16)
        weave()
    for i in range(ni):
        res = _dot(aak_ref[i], _block_diag(tile(v_ref, i)))
        av_ref[i] = res[:CHUNK].astype(BF16)
        rkv_ref[i] = res[CHUNK:]
        weave()
    for i in range(ni):
        n1 = np_ref[i]
        np_ref[i] = _dot(n1, _block_diag(n1)).astype(BF16)
        weave()
    for _ in range(4):
        for i in range(ni):
            npow = np_ref[i]
            tm = tm_ref[i]
            res = _dot(jnp.concatenate([tm.astype(BF16), npow], axis=0), _block_diag(npow))
            tm_ref[i] = tm + res[:CHUNK]
            np_ref[i] = res[CHUNK:].astype(BF16)
            weave()
    for i in range(ni):
        tm = tm_ref[i]
        tmb_ref[i] = (tm + _dot(tm.astype(BF16), _block_diag(np_ref[i]))).astype(BF16)
        weave()
    for i in range(ni):
        rhs = jnp.concatenate([_block_diag(tile(aa_ref, i)), _block_diag(av_ref[i])], axis=1)
        w_ref[i] = _dot(tmb_ref[i], rhs).astype(BF16)
        weave()
    while pending:
        pending.pop(0)()
    while preps:
        preps.pop(0)()
    for i in range(ni):
        c, _, c0 = inst[i]
        w12 = w_ref[i]
        rhs = jnp.concatenate([_block_diag(w12[:, :QUAD]), _block_diag(w12[:, QUAD:])], axis=1)
        res = _dot(jnp.concatenate([arb_ref[i], bht_ref[i]], axis=0), rhs)
        pc = pc_ref[rslot, c:c + 1, c0:c0 + QUAD]
        qm_ref[i] = (tile(ra_ref, i).astype(F32) + res[:CHUNK, :QUAD]).astype(BF16)
        yi_ref[i] = res[:CHUNK, QUAD:] + rkv_ref[i, :CHUNK]
        mq_ref[i] = (ident * pc + res[CHUNK:, :QUAD]).astype(BF16)
        nq_ref[i] = res[CHUNK:, QUAD:] + rkv_ref[i, CHUNK:]


def _rwkv_mix(z, shift_prev, s0q, mu, w0, wa_bf, a0, g2_bf, k_k, k_a, r_k, lnx_g, lnx_b, tt):
    B, T, _ = z.shape
    nq = D_RWKV // QUAD
    ni = (tt // CHUNK) * nq
    idx = jnp.arange(QUAD) // HEAD_DIM
    ones = (idx[:, None] == idx[None, :]).astype(BF16)
    pb = min(tt, 64)
    ti = jnp.arange(pb)
    tri = ((ti[:, None] // CHUNK == ti[None, :] // CHUNK) & (ti[None, :] <= ti[:, None])).astype(BF16)
    nt = T // tt
    ntiles = B * nt
    cur = lambda s: jnp.minimum(s, ntiles - 1)
    prev = lambda s: jnp.maximum(s - 2, 0)
    row = lambda s: (0, 0)
    vec = pl.BlockSpec((1, D_RWKV), row)
    return pl.pallas_call(
        functools.partial(_rwkv_kernel, tt=tt, nt=nt, ntiles=ntiles, pb=pb),
        grid=(ntiles + 2,),
        in_specs=[
            pl.BlockSpec((1, tt, D_SHIFT), lambda s: (cur(s) // nt, cur(s) % nt, 0)),
            pl.BlockSpec((1, 1, D_SHIFT), lambda s: (cur(s) // nt, 0, 0)),
            pl.BlockSpec((1, nq, HEAD_DIM, QUAD), lambda s: (prev(s) // nt, 0, 0, 0)),
            pl.BlockSpec((1, D_SHIFT), row),
            vec,
            pl.BlockSpec((LORA_W + LORA_A, 2 * D_RWKV), row),
            vec,
            pl.BlockSpec((LORA_G, D_RWKV), row),
            vec, vec, vec, vec, vec,
            pl.BlockSpec((QUAD, QUAD), row),
            pl.BlockSpec((pb, pb), row),
        ],
        out_specs=[
            pl.BlockSpec((1, tt, D_RWKV), lambda s: (prev(s) // nt, prev(s) % nt, 0)),
            pl.BlockSpec((1, 1, D_SHIFT), lambda s: (cur(s) // nt, 0, 0)),
            pl.BlockSpec((1, nq, HEAD_DIM, QUAD), lambda s: (prev(s) // nt, 0, 0, 0)),
        ],
        out_shape=[
            jax.ShapeDtypeStruct((B, T, D_RWKV), BF16),
            jax.ShapeDtypeStruct((B, 1, D_SHIFT), F32),
            jax.ShapeDtypeStruct((B, nq, HEAD_DIM, QUAD), F32),
        ],
        scratch_shapes=[
            pltpu.VMEM((1, D_SHIFT), F32),
            pltpu.VMEM((nq, HEAD_DIM, QUAD), F32),
            pltpu.VMEM((2, tt, D_RWKV), BF16),
            pltpu.VMEM((2, tt, D_RWKV), BF16),
            pltpu.VMEM((2, tt, D_RWKV), BF16),
            pltpu.VMEM((2, tt, D_RWKV), BF16),
            pltpu.VMEM((2, tt, D_RWKV), BF16),
            pltpu.VMEM((2, tt // CHUNK, D_RWKV), F32),
            pltpu.VMEM((tt, D_RWKV), F32),
            pltpu.VMEM((ni, CHUNK, QUAD), F32),
            pltpu.VMEM((ni, CHUNK, QUAD), BF16),
            pltpu.VMEM((ni, CHUNK, QUAD), BF16),
            pltpu.VMEM((ni, 3 * CHUNK, QUAD), BF16),
            pltpu.VMEM((ni, CHUNK, QUAD), BF16),
            pltpu.VMEM((ni, CHUNK, QUAD), BF16),
            pltpu.VMEM((ni, CHUNK, QUAD), BF16),
            pltpu.VMEM((ni, 2 * CHUNK, QUAD), F32),
            pltpu.VMEM((ni, CHUNK, 2 * QUAD), BF16),
            pltpu.VMEM((ni, CHUNK, QUAD), BF16),
            pltpu.VMEM((ni, CHUNK, QUAD), BF16),
            pltpu.VMEM((ni, CHUNK, QUAD), F32),
            pltpu.VMEM((ni, CHUNK, QUAD), F32),
            pltpu.VMEM((3, tt, D_RWKV), F32),
            pltpu.VMEM((3, tt, D_RWKV), F32),
        ],
        compiler_params=pltpu.CompilerParams(
            dimension_semantics=("arbitrary",), vmem_limit_bytes=VMEM_LIMIT),
        name="rwkv_mix",
    )(z, shift_prev, s0q, mu, w0, wa_bf, a0, g2_bf, k_k, k_a, r_k, lnx_g, lnx_b, ones, tri)


def _out_ffn_kernel(x_ref, co_ref, ro_ref, mod_ref, gpost_ref, gfpre_ref, gfpost_ref,
                    wo_ref, wg_ref, wu_ref, wd_ref, y_ref):
    m = mod_ref[0]
    tm = x_ref.shape[1]
    nsplit = 2 if tm >= 256 else 1
    rows = [slice(i * tm // nsplit, (i + 1) * tm // nsplit) for i in range(nsplit)]

    def mix(r):
        return _dot(co_ref[0, r, :], wo_ref[:D_CONV, :]) + _dot(ro_ref[0, r, :], wo_ref[D_CONV:, :])

    def resid(r, mx):
        return x_ref[0, r, :] + (1.0 + m[2:3]) * _rms(mx, gpost_ref[...])

    def pre(x1):
        return (_rms(x1, gfpre_ref[...]) * (1.0 + m[4:5]) + m[3:4]).astype(BF16)

    def hidden(h):
        gate = _dot(h, wg_ref[...])
        up = _dot(h, wu_ref[...])
        return (gate * _sigmoid(gate) * up).astype(BF16)

    mixes = [mix(r) for r in rows]
    x1 = [None] * nsplit
    act = [None] * nsplit
    for i, r in enumerate(rows):
        x1[i] = resid(r, mixes[i])
        act[i] = hidden(pre(x1[i]))
    for i, r in enumerate(rows):
        f = _dot(act[i], wd_ref[...])
        y_ref[0, r, :] = x1[i] + (1.0 + m[5:6]) * _rms(f, gfpost_ref[...])


def _out_ffn(x, conv_out, rwkv_out, mod, g_post, g_fpre, g_fpost, wo, wg, wu, wd, tm):
    B, T, _ = x.shape
    d_ff = wg.shape[1]
    row = lambda b, t: (0, 0)
    once = dict(pipeline_mode=pl.Buffered(1))
    return pl.pallas_call(
        _out_ffn_kernel,
        grid=(B, T // tm),
        in_specs=[
            pl.BlockSpec((1, tm, D_MODEL), lambda b, t: (b, t, 0)),
            pl.BlockSpec((1, tm, D_CONV), lambda b, t: (b, t, 0)),
            pl.BlockSpec((1, tm, D_RWKV), lambda b, t: (b, t, 0)),
            pl.BlockSpec((1, 6, D_MODEL), lambda b, t: (b, 0, 0)),
            pl.BlockSpec((1, D_MODEL), row),
            pl.BlockSpec((1, D_MODEL), row),
            pl.BlockSpec((1, D_MODEL), row),
            pl.BlockSpec((D_MODEL, D_MODEL), row, **once),
            pl.BlockSpec((D_MODEL, d_ff), row, **once),
            pl.BlockSpec((D_MODEL, d_ff), row, **once),
            pl.BlockSpec((d_ff, D_MODEL), row, **once),
        ],
        out_specs=pl.BlockSpec((1, tm, D_MODEL), lambda b, t: (b, t, 0)),
        out_shape=jax.ShapeDtypeStruct((B, T, D_MODEL), F32),
        compiler_params=pltpu.CompilerParams(
            dimension_semantics=("arbitrary", "arbitrary"), vmem_limit_bytes=VMEM_LIMIT),
        name="out_ffn",
    )(x, conv_out, rwkv_out, mod, g_post, g_fpre, g_fpost, wo, wg, wu, wd)


def _state_to_quads(s):
    B = s.shape[0]
    s = s.reshape(B, N_HEADS // 4, 4, HEAD_DIM, HEAD_DIM)
    return s.transpose(0, 1, 4, 2, 3).reshape(B, N_HEADS // 4, HEAD_DIM, QUAD)


def _quads_to_state(sq):
    B = sq.shape[0]
    s = sq.reshape(B, N_HEADS // 4, HEAD_DIM, 4, HEAD_DIM)
    return s.transpose(0, 1, 3, 4, 2).reshape(B, N_HEADS, HEAD_DIM, HEAD_DIM)


def _layer(x, mod, conv_buf, shift_prev, wkv, p, tiles):
    tm, tr = tiles
    z, conv_out, new_buf = _inproj_conv(x, mod, p["g_mix_pre"], p["w_in"], conv_buf, p["conv_dw"],
                                        p["conv_b"], p["conv_ln_g"], p["conv_ln_b"], tm)
    rwkv_out, new_shift, sq = _rwkv_mix(
        z, shift_prev[:, None, :], _state_to_quads(wkv), p["mu_shift"], p["w0"], p["wa"], p["a0"],
        p["g2"], p["k_k"], p["k_a"], p["r_k"], p["lnx_g"], p["lnx_b"], tr)
    y = _out_ffn(x, conv_out, rwkv_out, mod, p["g_mix_post"], p["g_ffn_pre"], p["g_ffn_post"],
                 p["w_out"], p["w_gate"], p["w_up"], p["w_down"], tm)
    return y, new_buf, new_shift[:, 0, :], _quads_to_state(sq)


def _layer_params(l, w):
    row = lambda a: a[l].reshape(1, -1)
    wa = jnp.zeros((LORA_W + LORA_A, 2 * D_RWKV), F32)
    wa = wa.at[:LORA_W, :D_RWKV].set(w["w2"][l]).at[LORA_W:, D_RWKV:].set(w["a2"][l])
    return {
        "g_mix_pre": row(w["g_mix_pre"]), "g_mix_post": row(w["g_mix_post"]),
        "g_ffn_pre": row(w["g_ffn_pre"]), "g_ffn_post": row(w["g_ffn_post"]),
        "w_in": w["w_in"][l].astype(BF16),
        "conv_dw": w["conv_dw"][l], "conv_b": row(w["conv_b"]),
        "conv_ln_g": row(w["conv_ln_g"]), "conv_ln_b": row(w["conv_ln_b"]),
        "mu_shift": row(w["mu_shift"]), "w0": row(w["w0"]), "wa": wa.astype(BF16),
        "a0": row(w["a0"]), "g2": w["g2"][l].astype(BF16),
        "k_k": row(w["k_k"]), "k_a": row(w["k_a"]), "r_k": row(w["r_k"]),
        "lnx_g": row(w["lnx_g"]), "lnx_b": row(w["lnx_b"]),
        "w_out": w["w_out"][l].astype(BF16), "w_gate": w["w_gate"][l].astype(BF16),
        "w_up": w["w_up"][l].astype(BF16), "w_down": w["w_down"][l].astype(BF16),
    }


def _tiles(T):
    tm = min(T, 512)
    tr = min(T, 512)
    return tm, tr


def kernel(x_prompt, x_sample, cache_conv, state_shift, state_wkv, c_prompt, c_sample, w_mod, b_mod, g_mix_pre, g_mix_post, g_ffn_pre, g_ffn_post, w_in, conv_dw, conv_b, conv_ln_g, conv_ln_b, mu_shift, w0, w2, a0, a2, g2, k_k, k_a, r_k, lnx_g, lnx_b, w_out, w_gate, w_up, w_down):
    w = dict(g_mix_pre=g_mix_pre, g_mix_post=g_mix_post, g_ffn_pre=g_ffn_pre, g_ffn_post=g_ffn_post,
             w_in=w_in, conv_dw=conv_dw, conv_b=conv_b, conv_ln_g=conv_ln_g, conv_ln_b=conv_ln_b,
             mu_shift=mu_shift, w0=w0, w2=w2, a0=a0, a2=a2, g2=g2, k_k=k_k, k_a=k_a, r_k=r_k,
             lnx_g=lnx_g, lnx_b=lnx_b, w_out=w_out, w_gate=w_gate, w_up=w_up, w_down=w_down)
    depth = w_mod.shape[0]
    bp, tp, _ = x_prompt.shape
    bs, ts, _ = x_sample.shape
    mod = _modulation(jnp.concatenate([c_prompt, c_sample], axis=0), w_mod, b_mod)
    mod = mod.reshape(depth, bp + bs, 6, D_MODEL)
    zero_conv = jnp.zeros((bp, CONV_BUF, D_CONV), F32)
    zero_shift = jnp.zeros((bp, D_SHIFT), F32)
    zero_wkv = jnp.zeros((bp, N_HEADS, HEAD_DIM, HEAD_DIM), F32)
    yp, ys = x_prompt, x_sample
    cp, sp, wp, cs, ss, wsm = [], [], [], [], [], []
    for l in range(depth):
        p = _layer_params(l, w)
        yp, b1, s1, k1 = _layer(yp, mod[l, :bp], zero_conv, zero_shift, zero_wkv, p, _tiles(tp))
        ys, b2, s2, k2 = _layer(ys, mod[l, bp:], cache_conv[l], state_shift[l], state_wkv[l], p,
                                _tiles(ts))
        cp.append(b1); sp.append(s1); wp.append(k1)
        cs.append(b2); ss.append(s2); wsm.append(k2)
    return (yp, ys, jnp.stack(cp), jnp.stack(sp), jnp.stack(wp),
            jnp.stack(cs), jnp.stack(ss), jnp.stack(wsm))
```

```python
import functools
import math

import jax
import jax.numpy as jnp
from jax import lax
from jax.experimental import pallas as pl
from jax.experimental.pallas import tpu as pltpu

F32 = jnp.float32
BF16 = jnp.bfloat16

D_MODEL = 1024
D_CONV = 512
D_RWKV = 512
HEAD_DIM = 64
N_HEADS = 8
CONV_WIDTH = 31
CONV_BUF = 30
LORA_W = 64
LORA_A = 64
LORA_G = 128
D_SHIFT = 3 * D_RWKV + LORA_W + LORA_A + LORA_G
D_IN = 2 * D_CONV + D_SHIFT
RMS_EPS = 1e-6
LN_EPS = 1e-5
GN_EPS = 64e-5
L2_EPS = 1e-12

CHUNK = 64
QUAD = 4 * HEAD_DIM
CONV_HALO = 32
VMEM_LIMIT = 56 * 1024 * 1024


def _dot(a, b):
    return jnp.dot(a, b, preferred_element_type=F32)


def _dot_nt(a, b):
    return lax.dot_general(a, b, (((1,), (1,)), ((), ())), preferred_element_type=F32)


def _dot_hilo_lhs(m, x):
    hi = x.astype(BF16)
    lo = (x - hi.astype(F32)).astype(BF16)
    return _dot(m, hi) + _dot(m, lo)


def _head_sum(x, ones):
    xb = x.astype(BF16)
    return jnp.concatenate(
        [_dot(xb[:, q * QUAD:(q + 1) * QUAD], ones) for q in range(D_RWKV // QUAD)], axis=1)


def _sigmoid(x):
    return 1.0 / (1.0 + jnp.exp(-x))


def _rms(x, g):
    return x * lax.rsqrt(jnp.mean(x * x, axis=-1, keepdims=True) + RMS_EPS) * g


def _mod_kernel(c_ref, w_ref, b_ref, o_ref):
    c = c_ref[...]
    s = c * _sigmoid(c)
    o_ref[0] = jnp.dot(s, w_ref[0], preferred_element_type=F32,
                       precision=lax.Precision.HIGHEST) + b_ref[0]


def _modulation(c_all, w_mod, b_mod):
    L = w_mod.shape[0]
    bc = c_all.shape[0]
    n = w_mod.shape[2]
    tn = 1024
    return pl.pallas_call(
        _mod_kernel,
        grid=(L, n // tn),
        in_specs=[
            pl.BlockSpec((bc, D_MODEL), lambda l, j: (0, 0)),
            pl.BlockSpec((1, D_MODEL, tn), lambda l, j: (l, 0, j)),
            pl.BlockSpec((1, 1, tn), lambda l, j: (l, 0, j)),
        ],
        out_specs=pl.BlockSpec((1, bc, tn), lambda l, j: (l, 0, j)),
        out_shape=jax.ShapeDtypeStruct((L, bc, n), F32),
        compiler_params=pltpu.CompilerParams(
            dimension_semantics=("arbitrary", "arbitrary"), vmem_limit_bytes=VMEM_LIMIT),
        name="modulation",
    )(c_all, w_mod, b_mod.reshape(L, 1, n))


def _inproj_conv_kernel(x_ref, mod_ref, g_ref, w_ref, buf_ref, dw_ref, cb_ref, lg_ref, lb_ref,
                        z_ref, out_ref, nbuf_ref, win_ref, *, tt, rb):
    first = pl.program_id(1) == 0
    pad = CONV_HALO - CONV_BUF
    strip = D_CONV // 2

    m = mod_ref[0]
    hb = (_rms(x_ref[0], g_ref[...]) * (1.0 + m[1:2]) + m[0:1]).astype(BF16)
    u = _dot(hb, w_ref[:, :2 * D_CONV])
    win_ref[0:CONV_HALO, :] = jnp.where(first, buf_ref[0], win_ref[0:CONV_HALO, :])
    win_ref[CONV_HALO:CONV_HALO + tt, :] = u[:, :D_CONV] * _sigmoid(u[:, D_CONV:])
    z_ref[0] = _dot(hb, w_ref[:, 2 * D_CONV:])

    for r in range(tt // rb):
        base = r * rb
        accs = []
        for c0 in range(0, D_CONV, strip):
            cols = slice(c0, c0 + strip)
            acc = jnp.broadcast_to(cb_ref[:, cols], (rb, strip))
            for s in range(8):
                part = None
                for a in range((pad + CONV_WIDTH + 7) // 8):
                    j = 8 * a + s
                    if j < pad or j >= pad + CONV_WIDTH:
                        continue
                    lo = base + 8 * a
                    term = win_ref[lo:lo + rb + (8 if s else 0), cols] * dw_ref[j - pad:j - pad + 1, cols]
                    part = term if part is None else part + term
                acc = acc + part[s:s + rb]
            accs.append(acc)
        acc = jnp.concatenate(accs, axis=1)
        mu = jnp.mean(acc, axis=-1, keepdims=True)
        d = acc - mu
        var = jnp.mean(d * d, axis=-1, keepdims=True)
        y = d * lax.rsqrt(var + LN_EPS) * lg_ref[...] + lb_ref[...]
        out_ref[0, base:base + rb, :] = (y * _sigmoid(y)).astype(out_ref.dtype)

    tail = win_ref[tt:tt + CONV_HALO, :]
    win_ref[0:CONV_HALO, :] = tail
    nbuf_ref[0] = tail


def _inproj_conv(x, mod, g, w_in_bf, buf, dw, cb, lg, lb, tt):
    B, T, _ = x.shape
    rb = min(tt, 64)
    pad = CONV_HALO - CONV_BUF
    buf = jnp.pad(buf, ((0, 0), (pad, 0), (0, 0)))
    row = lambda b, t: (0, 0)
    z, out, nbuf = pl.pallas_call(
        functools.partial(_inproj_conv_kernel, tt=tt, rb=rb),
        grid=(B, T // tt),
        in_specs=[
            pl.BlockSpec((1, tt, D_MODEL), lambda b, t: (b, t, 0)),
            pl.BlockSpec((1, 6, D_MODEL), lambda b, t: (b, 0, 0)),
            pl.BlockSpec((1, D_MODEL), row),
            pl.BlockSpec((D_MODEL, D_IN), row),
            pl.BlockSpec((1, CONV_HALO, D_CONV), lambda b, t: (b, 0, 0)),
            pl.BlockSpec((CONV_WIDTH, D_CONV), row),
            pl.BlockSpec((1, D_CONV), row),
            pl.BlockSpec((1, D_CONV), row),
            pl.BlockSpec((1, D_CONV), row),
        ],
        out_specs=[
            pl.BlockSpec((1, tt, D_SHIFT), lambda b, t: (b, t, 0)),
            pl.BlockSpec((1, tt, D_CONV), lambda b, t: (b, t, 0)),
            pl.BlockSpec((1, CONV_HALO, D_CONV), lambda b, t: (b, 0, 0)),
        ],
        out_shape=[
            jax.ShapeDtypeStruct((B, T, D_SHIFT), F32),
            jax.ShapeDtypeStruct((B, T, D_CONV), BF16),
            jax.ShapeDtypeStruct((B, CONV_HALO, D_CONV), F32),
        ],
        scratch_shapes=[pltpu.VMEM((CONV_HALO + tt, D_CONV), F32)],
        compiler_params=pltpu.CompilerParams(
            dimension_semantics=("arbitrary", "arbitrary"), vmem_limit_bytes=VMEM_LIMIT),
        name="inproj_conv",
    )(x, mod, g, w_in_bf, buf, dw, cb, lg, lb)
    return z, out, nbuf[:, pad:, :]


def _block_diag(x):
    half = QUAD // 2
    lo = lax.broadcasted_iota(jnp.int32, (HEAD_DIM, half), 1) < HEAD_DIM
    zero = jnp.zeros((HEAD_DIM, half), x.dtype)
    xl, xr = x[:, :half], x[:, half:]
    rows = [
        jnp.concatenate([jnp.where(lo, xl, zero), zero], axis=1),
        jnp.concatenate([jnp.where(lo, zero, xl), zero], axis=1),
        jnp.concatenate([zero, jnp.where(lo, xr, zero)], axis=1),
        jnp.concatenate([zero, jnp.where(lo, zero, xr)], axis=1),
    ]
    return jnp.concatenate(rows, axis=0)


def _rwkv_kernel(z_ref, sprev_ref, s0_ref, mu_ref, w0_ref, wa_ref, a0_ref, g2_ref, kk_ref, ka_ref,
                 rk_ref, lg_ref, lb_ref, ones_ref, tri_ref,
                 out_ref, nshift_ref, sout_ref,
                 carry_ref, sq_ref, ra_ref, aa_ref, bb_ref, kt_ref, v_ref,
                 pc_ref, y_ref, tm_ref, np_ref, arb_ref, aak_ref, bht_ref, tmb_ref, av_ref,
                 rkv_ref, w_ref, qm_ref, mq_ref, nq_ref, yi_ref, bonus_ref, gate_ref,
                 *, tt, nt, ntiles, pb):
    step = pl.program_id(0)
    first = jnp.minimum(step, ntiles - 1) % nt == 0
    first_prev = jnp.maximum(step - 2, 0) % nt == 0
    wslot = step % 2
    rslot = 1 - wslot
    bwslot = step % 3
    brslot = (step + 1) % 3
    nq = D_RWKV // QUAD
    nchunk = tt // CHUNK
    inst = [(c, c * CHUNK, q * QUAD) for c in range(nchunk) for q in range(nq)]
    ni = len(inst)
    ones = ones_ref[...]

    def prep(r0):
        z = z_ref[0, r0:r0 + pb, :]
        if r0 == 0:
            last_row = jnp.where(first, sprev_ref[0], carry_ref[...])
        else:
            last_row = z_ref[0, r0 - 1:r0, :]
        rows = lax.broadcasted_iota(jnp.int32, (pb, 1), 0)
        z_prev = jnp.where(rows == 0, last_row, pltpu.roll(z, 1, 0))
        if r0 + pb == tt:
            carry_ref[...] = z[pb - 1:pb, :]
            nshift_ref[0] = z[pb - 1:pb, :]
        zm = z + (z_prev - z) * mu_ref[...]
        r = zm[:, 0:D_RWKV]
        k = zm[:, D_RWKV:2 * D_RWKV]
        v = zm[:, 2 * D_RWKV:3 * D_RWKV]
        zwa = zm[:, 3 * D_RWKV:3 * D_RWKV + LORA_W + LORA_A]
        zg = zm[:, 3 * D_RWKV + LORA_W + LORA_A:]
        lane = lax.broadcasted_iota(jnp.int32, zwa.shape, 1)
        zwa = jnp.where(lane < LORA_W, jnp.tanh(zwa), zwa)
        wa = _dot(zwa.astype(BF16), wa_ref[...])
        lw = -math.exp(-0.5) * _sigmoid(w0_ref[...] + wa[:, :D_RWKV])
        a = _sigmoid(a0_ref[...] + wa[:, D_RWKV:])
        g = _dot(_sigmoid(zg).astype(BF16), g2_ref[...])
        kk = k * kk_ref[...]
        k = k * (1.0 + (a - 1.0) * ka_ref[...])
        sums = _head_sum(jnp.concatenate([kk * kk, r * k * rk_ref[...]], axis=0), ones)
        kk = kk * lax.rsqrt(sums[:pb] + L2_EPS)
        b = kk * a
        bonus = sums[pb:] * v
        cum = _dot_hilo_lhs(tri_ref[...], lw)
        e_neg = jnp.exp(-cum)
        rs = slice(r0, r0 + pb)
        ra_ref[wslot, rs, :] = (r * jnp.exp(cum)).astype(BF16)
        aa_ref[wslot, rs, :] = (-kk * jnp.exp(cum - lw)).astype(BF16)
        bb_ref[wslot, rs, :] = (b * e_neg).astype(BF16)
        kt_ref[wslot, rs, :] = (k * e_neg).astype(BF16)
        v_ref[wslot, rs, :] = v.astype(BF16)
        for c in range(pb // CHUNK):
            cc = r0 // CHUNK + c
            pc_ref[wslot, cc:cc + 1, :] = jnp.exp(cum[(c + 1) * CHUNK - 1:(c + 1) * CHUNK, :])
        bonus_ref[bwslot, rs, :] = bonus
        gate_ref[bwslot, rs, :] = g

    state = [jnp.where(first_prev, s0_ref[0, q], sq_ref[q]) for q in range(nq)]

    def state_step(i):
        c, r0, c0 = inst[i]
        q = i % nq
        res = _dot(jnp.concatenate([qm_ref[i], mq_ref[i]], axis=0),
                   _block_diag(state[q].astype(BF16)))
        y_ref[r0:r0 + CHUNK, c0:c0 + QUAD] = yi_ref[i] + res[:CHUNK]
        state[q] = res[CHUNK:] + nq_ref[i]
        if c == nchunk - 1:
            sq_ref[q] = state[q]
            sout_ref[0, q] = state[q]

    def finish_prev():
        y = y_ref[...]
        mean = _head_sum(y, ones) * (1.0 / HEAD_DIM)
        d = y - mean
        var = _head_sum(d * d, ones) * (1.0 / HEAD_DIM)
        yn = d * lax.rsqrt(var + GN_EPS) * lg_ref[...] + lb_ref[...]
        out_ref[0] = ((yn + bonus_ref[brslot]) * gate_ref[brslot]).astype(out_ref.dtype)

    pending = [functools.partial(state_step, i) for i in range(ni)] + [finish_prev]
    preps = [functools.partial(prep, r0) for r0 in range(0, tt, pb)]
    n_iter = 9 * ni
    sq_lo, sq_hi = 2 * ni, 8 * ni
    emitted = [0]

    def weave():
        emitted[0] += 1
        while pending and (ni + 1 - len(pending)) * n_iter < emitted[0] * (ni + 1):
            pending.pop(0)()
        done = tt // pb - len(preps)
        while preps and emitted[0] > sq_lo and done * (sq_hi - sq_lo) < (emitted[0] - sq_lo) * (tt // pb):
            preps.pop(0)()
            done += 1

    rowc = lax.broadcasted_iota(jnp.int32, (CHUNK, QUAD), 0)
    colc = lax.broadcasted_iota(jnp.int32, (CHUNK, QUAD), 1) % CHUNK
    strict = colc < rowc
    incl = colc <= rowc
    ident = jnp.where(colc == rowc, 1.0, 0.0)

    def tile(ref, i):
        _, r0, c0 = inst[i]
        return ref[rslot, r0:r0 + CHUNK, c0:c0 + QUAD]

    for i in range(ni):
        c, r0, c0 = inst[i]
        pc = pc_ref[rslot, c:c + 1, c0:c0 + QUAD]
        lhs = jnp.concatenate([tile(aa_ref, i), tile(ra_ref, i), (ident * pc).astype(BF16)], axis=0)
        sb = _dot_nt(lhs, _block_diag(tile(bb_ref, i)))
        sk = _dot_nt(lhs, _block_diag(tile(kt_ref, i)))
        a_ab = jnp.where(strict, sb[:CHUNK], 0.0)
        tm_ref[i] = ident + a_ab
        np_ref[i] = a_ab.astype(BF16)
        arb_ref[i] = jnp.where(incl, sb[CHUNK:2 * CHUNK], 0.0).astype(BF16)
        bht_ref[i] = sb[2 * CHUNK:].astype(BF16)
        aak_ref[i, 0:CHUNK] = jnp.where(strict, sk[:CHUNK], 0.0).astype(BF16)
        aak_ref[i, CHUNK:2 * CHUNK] = jnp.where(incl, sk[CHUNK:2 * CHUNK], 0.0).astype(BF16)
        aak_ref[i, 2 * CHUNK:] = sk[2 * CHUNK:].astype(BF16)
        weave()
    for i in range(ni):
        res = _dot(aak_ref[i], _block_diag(tile(v_ref, i)))
        av_ref[i] = res[:CHUNK].astype(BF16)
        rkv_ref[i] = res[CHUNK:]
        weave()
    for i in range(ni):
        n1 = np_ref[i]
        np_ref[i] = _dot(n1, _block_diag(n1)).astype(BF16)
        weave()
    for _ in range(4):
        for i in range(ni):
            npow = np_ref[i]
            tm = tm_ref[i]
            res = _dot(jnp.concatenate([tm.astype(BF16), npow], axis=0), _block_diag(npow))
            tm_ref[i] = tm + res[:CHUNK]
            np_ref[i] = res[CHUNK:].astype(BF16)
            weave()
    for i in range(ni):
        tm = tm_ref[i]
        tmb_ref[i] = (tm + _dot(tm.astype(BF16), _block_diag(np_ref[i]))).astype(BF16)
        weave()
    for i in range(ni):
        rhs = jnp.concatenate([_block_diag(tile(aa_ref, i)), _block_diag(av_ref[i])], axis=1)
        w_ref[i] = _dot(tmb_ref[i], rhs).astype(BF16)
        weave()
    while pending:
        pending.pop(0)()
    while preps:
        preps.pop(0)()
    for i in range(ni):
        c, _, c0 = inst[i]
        w12 = w_ref[i]
        rhs = jnp.concatenate([_block_diag(w12[:, :QUAD]), _block_diag(w12[:, QUAD:])], axis=1)
        res = _dot(jnp.concatenate([arb_ref[i], bht_ref[i]], axis=0), rhs)
        pc = pc_ref[rslot, c:c + 1, c0:c0 + QUAD]
        qm_ref[i] = (tile(ra_ref, i).astype(F32) + res[:CHUNK, :QUAD]).astype(BF16)
        yi_ref[i] = res[:CHUNK, QUAD:] + rkv_ref[i, :CHUNK]
        mq_ref[i] = (ident * pc + res[CHUNK:, :QUAD]).astype(BF16)
        nq_ref[i] = res[CHUNK:, QUAD:] + rkv_ref[i, CHUNK:]


def _rwkv_mix(z, shift_prev, s0q, mu, w0, wa_bf, a0, g2_bf, k_k, k_a, r_k, lnx_g, lnx_b, tt):
    B, T, _ = z.shape
    nq = D_RWKV // QUAD
    ni = (tt // CHUNK) * nq
    idx = jnp.arange(QUAD) // HEAD_DIM
    ones = (idx[:, None] == idx[None, :]).astype(BF16)
    pb = min(tt, 64)
    ti = jnp.arange(pb)
    tri = ((ti[:, None] // CHUNK == ti[None, :] // CHUNK) & (ti[None, :] <= ti[:, None])).astype(BF16)
    nt = T // tt
    ntiles = B * nt
    cur = lambda s: jnp.minimum(s, ntiles - 1)
    prev = lambda s: jnp.maximum(s - 2, 0)
    row = lambda s: (0, 0)
    vec = pl.BlockSpec((1, D_RWKV), row)
    return pl.pallas_call(
        functools.partial(_rwkv_kernel, tt=tt, nt=nt, ntiles=ntiles, pb=pb),
        grid=(ntiles + 2,),
        in_specs=[
            pl.BlockSpec((1, tt, D_SHIFT), lambda s: (cur(s) // nt, cur(s) % nt, 0)),
            pl.BlockSpec((1, 1, D_SHIFT), lambda s: (cur(s) // nt, 0, 0)),
            pl.BlockSpec((1, nq, HEAD_DIM, QUAD), lambda s: (prev(s) // nt, 0, 0, 0)),
            pl.BlockSpec((1, D_SHIFT), row),
            vec,
            pl.BlockSpec((LORA_W + LORA_A, 2 * D_RWKV), row),
            vec,
            pl.BlockSpec((LORA_G, D_RWKV), row),
            vec, vec, vec, vec, vec,
            pl.BlockSpec((QUAD, QUAD), row),
            pl.BlockSpec((pb, pb), row),
        ],
        out_specs=[
            pl.BlockSpec((1, tt, D_RWKV), lambda s: (prev(s) // nt, prev(s) % nt, 0)),
            pl.BlockSpec((1, 1, D_SHIFT), lambda s: (cur(s) // nt, 0, 0)),
            pl.BlockSpec((1, nq, HEAD_DIM, QUAD), lambda s: (prev(s) // nt, 0, 0, 0)),
        ],
        out_shape=[
            jax.ShapeDtypeStruct((B, T, D_RWKV), BF16),
            jax.ShapeDtypeStruct((B, 1, D_SHIFT), F32),
            jax.ShapeDtypeStruct((B, nq, HEAD_DIM, QUAD), F32),
        ],
        scratch_shapes=[
            pltpu.VMEM((1, D_SHIFT), F32),
            pltpu.VMEM((nq, HEAD_DIM, QUAD), F32),
            pltpu.VMEM((2, tt, D_RWKV), BF16),
            pltpu.VMEM((2, tt, D_RWKV), BF16),
            pltpu.VMEM((2, tt, D_RWKV), BF16),
            pltpu.VMEM((2, tt, D_RWKV), BF16),
            pltpu.VMEM((2, tt, D_RWKV), BF16),
            pltpu.VMEM((2, tt // CHUNK, D_RWKV), F32),
            pltpu.VMEM((tt, D_RWKV), F32),
            pltpu.VMEM((ni, CHUNK, QUAD), F32),
            pltpu.VMEM((ni, CHUNK, QUAD), BF16),
            pltpu.VMEM((ni, CHUNK, QUAD), BF16),
            pltpu.VMEM((ni, 3 * CHUNK, QUAD), BF16),
            pltpu.VMEM((ni, CHUNK, QUAD), BF16),
            pltpu.VMEM((ni, CHUNK, QUAD), BF16),
            pltpu.VMEM((ni, CHUNK, QUAD), BF16),
            pltpu.VMEM((ni, 2 * CHUNK, QUAD), F32),
            pltpu.VMEM((ni, CHUNK, 2 * QUAD), BF16),
            pltpu.VMEM((ni, CHUNK, QUAD), BF16),
            pltpu.VMEM((ni, CHUNK, QUAD), BF16),
            pltpu.VMEM((ni, CHUNK, QUAD), F32),
            pltpu.VMEM((ni, CHUNK, QUAD), F32),
            pltpu.VMEM((3, tt, D_RWKV), F32),
            pltpu.VMEM((3, tt, D_RWKV), F32),
        ],
        compiler_params=pltpu.CompilerParams(
            dimension_semantics=("arbitrary",), vmem_limit_bytes=VMEM_LIMIT),
        name="rwkv_mix",
    )(z, shift_prev, s0q, mu, w0, wa_bf, a0, g2_bf, k_k, k_a, r_k, lnx_g, lnx_b, ones, tri)


def _out_ffn_kernel(x_ref, co_ref, ro_ref, mod_ref, gpost_ref, gfpre_ref, gfpost_ref,
                    wo_ref, wg_ref, wu_ref, wd_ref, y_ref):
    m = mod_ref[0]
    tm = x_ref.shape[1]
    nsplit = max(1, tm // 256)
    rows = [slice(i * tm // nsplit, (i + 1) * tm // nsplit) for i in range(nsplit)]

    def mix(r):
        return _dot(co_ref[0, r, :], wo_ref[:D_CONV, :]) + _dot(ro_ref[0, r, :], wo_ref[D_CONV:, :])

    def resid(r, mx):
        return x_ref[0, r, :] + (1.0 + m[2:3]) * _rms(mx, gpost_ref[...])

    def pre(x1):
        return (_rms(x1, gfpre_ref[...]) * (1.0 + m[4:5]) + m[3:4]).astype(BF16)

    def hidden(h):
        gate = _dot(h, wg_ref[...])
        up = _dot(h, wu_ref[...])
        return (gate * _sigmoid(gate) * up).astype(BF16)

    mixes = [mix(r) for r in rows]
    x1 = [None] * nsplit
    act = [None] * nsplit
    for i, r in enumerate(rows):
        x1[i] = resid(r, mixes[i])
        act[i] = hidden(pre(x1[i]))
    for i, r in enumerate(rows):
        f = _dot(act[i], wd_ref[...])
        y_ref[0, r, :] = x1[i] + (1.0 + m[5:6]) * _rms(f, gfpost_ref[...])


def _out_ffn(x, conv_out, rwkv_out, mod, g_post, g_fpre, g_fpost, wo, wg, wu, wd, tm):
    B, T, _ = x.shape
    d_ff = wg.shape[1]
    row = lambda b, t: (0, 0)
    once = dict(pipeline_mode=pl.Buffered(1))
    return pl.pallas_call(
        _out_ffn_kernel,
        grid=(B, T // tm),
        in_specs=[
            pl.BlockSpec((1, tm, D_MODEL), lambda b, t: (b, t, 0)),
            pl.BlockSpec((1, tm, D_CONV), lambda b, t: (b, t, 0)),
            pl.BlockSpec((1, tm, D_RWKV), lambda b, t: (b, t, 0)),
            pl.BlockSpec((1, 6, D_MODEL), lambda b, t: (b, 0, 0)),
            pl.BlockSpec((1, D_MODEL), row),
            pl.BlockSpec((1, D_MODEL), row),
            pl.BlockSpec((1, D_MODEL), row),
            pl.BlockSpec((D_MODEL, D_MODEL), row, **once),
            pl.BlockSpec((D_MODEL, d_ff), row, **once),
            pl.BlockSpec((D_MODEL, d_ff), row, **once),
            pl.BlockSpec((d_ff, D_MODEL), row, **once),
        ],
        out_specs=pl.BlockSpec((1, tm, D_MODEL), lambda b, t: (b, t, 0)),
        out_shape=jax.ShapeDtypeStruct((B, T, D_MODEL), F32),
        compiler_params=pltpu.CompilerParams(
            dimension_semantics=("arbitrary", "arbitrary"), vmem_limit_bytes=VMEM_LIMIT),
        name="out_ffn",
    )(x, conv_out, rwkv_out, mod, g_post, g_fpre, g_fpost, wo, wg, wu, wd)


def _state_to_quads(s):
    B = s.shape[0]
    s = s.reshape(B, N_HEADS // 4, 4, HEAD_DIM, HEAD_DIM)
    return s.transpose(0, 1, 4, 2, 3).reshape(B, N_HEADS // 4, HEAD_DIM, QUAD)


def _quads_to_state(sq):
    B = sq.shape[0]
    s = sq.reshape(B, N_HEADS // 4, HEAD_DIM, 4, HEAD_DIM)
    return s.transpose(0, 1, 3, 4, 2).reshape(B, N_HEADS, HEAD_DIM, HEAD_DIM)


def _layer(x, mod, conv_buf, shift_prev, wkv, p, tiles):
    tin, tr, tffn = tiles
    z, conv_out, new_buf = _inproj_conv(x, mod, p["g_mix_pre"], p["w_in"], conv_buf, p["conv_dw"],
                                        p["conv_b"], p["conv_ln_g"], p["conv_ln_b"], tin)
    rwkv_out, new_shift, sq = _rwkv_mix(
        z, shift_prev[:, None, :], _state_to_quads(wkv), p["mu_shift"], p["w0"], p["wa"], p["a0"],
        p["g2"], p["k_k"], p["k_a"], p["r_k"], p["lnx_g"], p["lnx_b"], tr)
    y = _out_ffn(x, conv_out, rwkv_out, mod, p["g_mix_post"], p["g_ffn_pre"], p["g_ffn_post"],
                 p["w_out"], p["w_gate"], p["w_up"], p["w_down"], tffn)
    return y, new_buf, new_shift[:, 0, :], _quads_to_state(sq)


def _layer_params(l, w):
    row = lambda a: a[l].reshape(1, -1)
    wa = jnp.zeros((LORA_W + LORA_A, 2 * D_RWKV), F32)
    wa = wa.at[:LORA_W, :D_RWKV].set(w["w2"][l]).at[LORA_W:, D_RWKV:].set(w["a2"][l])
    return {
        "g_mix_pre": row(w["g_mix_pre"]), "g_mix_post": row(w["g_mix_post"]),
        "g_ffn_pre": row(w["g_ffn_pre"]), "g_ffn_post": row(w["g_ffn_post"]),
        "w_in": w["w_in"][l].astype(BF16),
        "conv_dw": w["conv_dw"][l], "conv_b": row(w["conv_b"]),
        "conv_ln_g": row(w["conv_ln_g"]), "conv_ln_b": row(w["conv_ln_b"]),
        "mu_shift": row(w["mu_shift"]), "w0": row(w["w0"]), "wa": wa.astype(BF16),
        "a0": row(w["a0"]), "g2": w["g2"][l].astype(BF16),
        "k_k": row(w["k_k"]), "k_a": row(w["k_a"]), "r_k": row(w["r_k"]),
        "lnx_g": row(w["lnx_g"]), "lnx_b": row(w["lnx_b"]),
        "w_out": w["w_out"][l].astype(BF16), "w_gate": w["w_gate"][l].astype(BF16),
        "w_up": w["w_up"][l].astype(BF16), "w_down": w["w_down"][l].astype(BF16),
    }


def _tiles(T):
    tin = min(T, 512)
    tr = min(T, 512)
    tffn = min(T, 1024)
    return tin, tr, tffn


def kernel(x_prompt, x_sample, cache_conv, state_shift, state_wkv, c_prompt, c_sample, w_mod, b_mod, g_mix_pre, g_mix_post, g_ffn_pre, g_ffn_post, w_in, conv_dw, conv_b, conv_ln_g, conv_ln_b, mu_shift, w0, w2, a0, a2, g2, k_k, k_a, r_k, lnx_g, lnx_b, w_out, w_gate, w_up, w_down):
    w = dict(g_mix_pre=g_mix_pre, g_mix_post=g_mix_post, g_ffn_pre=g_ffn_pre, g_ffn_post=g_ffn_post,
             w_in=w_in, conv_dw=conv_dw, conv_b=conv_b, conv_ln_g=conv_ln_g, conv_ln_b=conv_ln_b,
             mu_shift=mu_shift, w0=w0, w2=w2, a0=a0, a2=a2, g2=g2, k_k=k_k, k_a=k_a, r_k=r_k,
             lnx_g=lnx_g, lnx_b=lnx_b, w_out=w_out, w_gate=w_gate, w_up=w_up, w_down=w_down)
    depth = w_mod.shape[0]
    bp, tp, _ = x_prompt.shape
    bs, ts, _ = x_sample.shape
    mod = _modulation(jnp.concatenate([c_prompt, c_sample], axis=0), w_mod, b_mod)
    mod = mod.reshape(depth, bp + bs, 6, D_MODEL)
    zero_conv = jnp.zeros((bp, CONV_BUF, D_CONV), F32)
    zero_shift = jnp.zeros((bp, D_SHIFT), F32)
    zero_wkv = jnp.zeros((bp, N_HEADS, HEAD_DIM, HEAD_DIM), F32)
    yp, ys = x_prompt, x_sample
    cp, sp, wp, cs, ss, wsm = [], [], [], [], [], []
    for l in range(depth):
        p = _layer_params(l, w)
        yp, b1, s1, k1 = _layer(yp, mod[l, :bp], zero_conv, zero_shift, zero_wkv, p, _tiles(tp))
        ys, b2, s2, k2 = _layer(ys, mod[l, bp:], cache_conv[l], state_shift[l], state_wkv[l], p,
                                _tiles(ts))
        cp.append(b1); sp.append(s1); wp.append(k1)
        cs.append(b2); ss.append(s2); wsm.append(k2)
    return (yp, ys, jnp.stack(cp), jnp.stack(sp), jnp.stack(wp),
            jnp.stack(cs), jnp.stack(ss), jnp.stack(wsm))
```

```python
import functools
import math

import jax
import jax.numpy as jnp
from jax import lax
from jax.experimental import pallas as pl
from jax.experimental.pallas import tpu as pltpu

F32 = jnp.float32
BF16 = jnp.bfloat16

D_MODEL = 1024
D_CONV = 512
D_RWKV = 512
HEAD_DIM = 64
N_HEADS = 8
CONV_WIDTH = 31
CONV_BUF = 30
LORA_W = 64
LORA_A = 64
LORA_G = 128
D_SHIFT = 3 * D_RWKV + LORA_W + LORA_A + LORA_G
D_IN = 2 * D_CONV + D_SHIFT
RMS_EPS = 1e-6
LN_EPS = 1e-5
GN_EPS = 64e-5
L2_EPS = 1e-12

CHUNK = 64
QUAD = 4 * HEAD_DIM
CONV_HALO = 32
VMEM_LIMIT = 56 * 1024 * 1024


def _dot(a, b):
    return jnp.dot(a, b, preferred_element_type=F32)


def _dot_nt(a, b):
    return lax.dot_general(a, b, (((1,), (1,)), ((), ())), preferred_element_type=F32)


def _dot_hilo_lhs(m, x):
    hi = x.astype(BF16)
    lo = (x - hi.astype(F32)).astype(BF16)
    return _dot(m, hi) + _dot(m, lo)


def _head_sum(x, ones):
    xb = x.astype(BF16)
    return jnp.concatenate(
        [_dot(xb[:, q * QUAD:(q + 1) * QUAD], ones) for q in range(D_RWKV // QUAD)], axis=1)


def _sigmoid(x):
    return 1.0 / (1.0 + jnp.exp(-x))


def _rms(x, g):
    return x * lax.rsqrt(jnp.mean(x * x, axis=-1, keepdims=True) + RMS_EPS) * g


def _mod_kernel(c_ref, w_ref, b_ref, o_ref):
    c = c_ref[...]
    s = c * _sigmoid(c)
    o_ref[0] = jnp.dot(s, w_ref[0], preferred_element_type=F32,
                       precision=lax.Precision.HIGHEST) + b_ref[0]


def _modulation(c_all, w_mod, b_mod):
    L = w_mod.shape[0]
    bc = c_all.shape[0]
    n = w_mod.shape[2]
    tn = 1024
    return pl.pallas_call(
        _mod_kernel,
        grid=(L, n // tn),
        in_specs=[
            pl.BlockSpec((bc, D_MODEL), lambda l, j: (0, 0)),
            pl.BlockSpec((1, D_MODEL, tn), lambda l, j: (l, 0, j)),
            pl.BlockSpec((1, 1, tn), lambda l, j: (l, 0, j)),
        ],
        out_specs=pl.BlockSpec((1, bc, tn), lambda l, j: (l, 0, j)),
        out_shape=jax.ShapeDtypeStruct((L, bc, n), F32),
        compiler_params=pltpu.CompilerParams(
            dimension_semantics=("arbitrary", "arbitrary"), vmem_limit_bytes=VMEM_LIMIT),
        name="modulation",
    )(c_all, w_mod, b_mod.reshape(L, 1, n))


def _inproj_conv_kernel(x_ref, mod_ref, g_ref, w_ref, buf_ref, dw_ref, cb_ref, lg_ref, lb_ref,
                        z_ref, out_ref, nbuf_ref, win_ref, *, tt, rb):
    first = pl.program_id(1) == 0
    pad = CONV_HALO - CONV_BUF
    strip = 128

    m = mod_ref[0]
    hb = (_rms(x_ref[0], g_ref[...]) * (1.0 + m[1:2]) + m[0:1]).astype(BF16)
    u = _dot(hb, w_ref[:, :2 * D_CONV])
    win_ref[0:CONV_HALO, :] = jnp.where(first, buf_ref[0], win_ref[0:CONV_HALO, :])
    win_ref[CONV_HALO:CONV_HALO + tt, :] = u[:, :D_CONV] * _sigmoid(u[:, D_CONV:])
    z_ref[0] = _dot(hb, w_ref[:, 2 * D_CONV:])

    for r in range(tt // rb):
        base = r * rb
        accs = []
        for c0 in range(0, D_CONV, strip):
            cols = slice(c0, c0 + strip)
            acc = jnp.broadcast_to(cb_ref[:, cols], (rb, strip))
            for s in range(8):
                part = None
                for a in range((pad + CONV_WIDTH + 7) // 8):
                    j = 8 * a + s
                    if j < pad or j >= pad + CONV_WIDTH:
                        continue
                    lo = base + 8 * a
                    term = win_ref[lo:lo + rb + (8 if s else 0), cols] * dw_ref[j - pad:j - pad + 1, cols]
                    part = term if part is None else part + term
                acc = acc + part[s:s + rb]
            accs.append(acc)
        acc = jnp.concatenate(accs, axis=1)
        mu = jnp.mean(acc, axis=-1, keepdims=True)
        d = acc - mu
        var = jnp.mean(d * d, axis=-1, keepdims=True)
        y = d * lax.rsqrt(var + LN_EPS) * lg_ref[...] + lb_ref[...]
        out_ref[0, base:base + rb, :] = (y * _sigmoid(y)).astype(out_ref.dtype)

    tail = win_ref[tt:tt + CONV_HALO, :]
    win_ref[0:CONV_HALO, :] = tail
    nbuf_ref[0] = tail


def _inproj_conv(x, mod, g, w_in_bf, buf, dw, cb, lg, lb, tt):
    B, T, _ = x.shape
    rb = min(tt, 128)
    pad = CONV_HALO - CONV_BUF
    buf = jnp.pad(buf, ((0, 0), (pad, 0), (0, 0)))
    row = lambda b, t: (0, 0)
    z, out, nbuf = pl.pallas_call(
        functools.partial(_inproj_conv_kernel, tt=tt, rb=rb),
        grid=(B, T // tt),
        in_specs=[
            pl.BlockSpec((1, tt, D_MODEL), lambda b, t: (b, t, 0)),
            pl.BlockSpec((1, 6, D_MODEL), lambda b, t: (b, 0, 0)),
            pl.BlockSpec((1, D_MODEL), row),
            pl.BlockSpec((D_MODEL, D_IN), row),
            pl.BlockSpec((1, CONV_HALO, D_CONV), lambda b, t: (b, 0, 0)),
            pl.BlockSpec((CONV_WIDTH, D_CONV), row),
            pl.BlockSpec((1, D_CONV), row),
            pl.BlockSpec((1, D_CONV), row),
            pl.BlockSpec((1, D_CONV), row),
        ],
        out_specs=[
            pl.BlockSpec((1, tt, D_SHIFT), lambda b, t: (b, t, 0)),
            pl.BlockSpec((1, tt, D_CONV), lambda b, t: (b, t, 0)),
            pl.BlockSpec((1, CONV_HALO, D_CONV), lambda b, t: (b, 0, 0)),
        ],
        out_shape=[
            jax.ShapeDtypeStruct((B, T, D_SHIFT), F32),
            jax.ShapeDtypeStruct((B, T, D_CONV), BF16),
            jax.ShapeDtypeStruct((B, CONV_HALO, D_CONV), F32),
        ],
        scratch_shapes=[pltpu.VMEM((CONV_HALO + tt, D_CONV), F32)],
        compiler_params=pltpu.CompilerParams(
            dimension_semantics=("arbitrary", "arbitrary"), vmem_limit_bytes=VMEM_LIMIT),
        name="inproj_conv",
    )(x, mod, g, w_in_bf, buf, dw, cb, lg, lb)
    return z, out, nbuf[:, pad:, :]


def _block_diag(x):
    half = QUAD // 2
    lo = lax.broadcasted_iota(jnp.int32, (HEAD_DIM, half), 1) < HEAD_DIM
    zero = jnp.zeros((HEAD_DIM, half), x.dtype)
    xl, xr = x[:, :half], x[:, half:]
    rows = [
        jnp.concatenate([jnp.where(lo, xl, zero), zero], axis=1),
        jnp.concatenate([jnp.where(lo, zero, xl), zero], axis=1),
        jnp.concatenate([zero, jnp.where(lo, xr, zero)], axis=1),
        jnp.concatenate([zero, jnp.where(lo, zero, xr)], axis=1),
    ]
    return jnp.concatenate(rows, axis=0)


def _rwkv_kernel(z_ref, sprev_ref, s0_ref, mu_ref, w0_ref, wa_ref, a0_ref, g2_ref, kk_ref, ka_ref,
                 rk_ref, lg_ref, lb_ref, ones_ref, tri_ref,
                 out_ref, nshift_ref, sout_ref,
                 carry_ref, sq_ref, ra_ref, aa_ref, bb_ref, kt_ref, v_ref,
                 pc_ref, y_ref, tm_ref, np_ref, arb_ref, aak_ref, bht_ref, tmb_ref, av_ref,
                 rkv_ref, w_ref, qm_ref, mq_ref, nq_ref, yi_ref, bonus_ref, gate_ref,
                 *, tt, nt, ntiles, pb):
    step = pl.program_id(0)
    first = jnp.minimum(step, ntiles - 1) % nt == 0
    first_prev = jnp.maximum(step - 2, 0) % nt == 0
    wslot = step % 2
    rslot = 1 - wslot
    bwslot = step % 3
    brslot = (step + 1) % 3
    nq = D_RWKV // QUAD
    nchunk = tt // CHUNK
    inst = [(c, c * CHUNK, q * QUAD) for c in range(nchunk) for q in range(nq)]
    ni = len(inst)
    ones = ones_ref[...]

    def prep(r0):
        z = z_ref[0, r0:r0 + pb, :]
        if r0 == 0:
            last_row = jnp.where(first, sprev_ref[0], carry_ref[...])
        else:
            last_row = z_ref[0, r0 - 1:r0, :]
        rows = lax.broadcasted_iota(jnp.int32, (pb, 1), 0)
        z_prev = jnp.where(rows == 0, last_row, pltpu.roll(z, 1, 0))
        if r0 + pb == tt:
            carry_ref[...] = z[pb - 1:pb, :]
            nshift_ref[0] = z[pb - 1:pb, :]
        zm = z + (z_prev - z) * mu_ref[...]
        r = zm[:, 0:D_RWKV]
        k = zm[:, D_RWKV:2 * D_RWKV]
        v = zm[:, 2 * D_RWKV:3 * D_RWKV]
        zwa = zm[:, 3 * D_RWKV:3 * D_RWKV + LORA_W + LORA_A]
        zg = zm[:, 3 * D_RWKV + LORA_W + LORA_A:]
        lane = lax.broadcasted_iota(jnp.int32, zwa.shape, 1)
        zwa = jnp.where(lane < LORA_W, jnp.tanh(zwa), zwa)
        wa = _dot(zwa.astype(BF16), wa_ref[...])
        lw = -math.exp(-0.5) * _sigmoid(w0_ref[...] + wa[:, :D_RWKV])
        a = _sigmoid(a0_ref[...] + wa[:, D_RWKV:])
        g = _dot(_sigmoid(zg).astype(BF16), g2_ref[...])
        kk = k * kk_ref[...]
        k = k * (1.0 + (a - 1.0) * ka_ref[...])
        sums = _head_sum(jnp.concatenate([kk * kk, r * k * rk_ref[...]], axis=0), ones)
        kk = kk * lax.rsqrt(sums[:pb] + L2_EPS)
        b = kk * a
        bonus = sums[pb:] * v
        cum = _dot_hilo_lhs(tri_ref[...], lw)
        e_neg = jnp.exp(-cum)
        rs = slice(r0, r0 + pb)
        ra_ref[wslot, rs, :] = (r * jnp.exp(cum)).astype(BF16)
        aa_ref[wslot, rs, :] = (-kk * jnp.exp(cum - lw)).astype(BF16)
        bb_ref[wslot, rs, :] = (b * e_neg).astype(BF16)
        kt_ref[wslot, rs, :] = (k * e_neg).astype(BF16)
        v_ref[wslot, rs, :] = v.astype(BF16)
        for c in range(pb // CHUNK):
            cc = r0 // CHUNK + c
            pc_ref[wslot, cc:cc + 1, :] = jnp.exp(cum[(c + 1) * CHUNK - 1:(c + 1) * CHUNK, :])
        bonus_ref[bwslot, rs, :] = bonus
        gate_ref[bwslot, rs, :] = g

    state = [jnp.where(first_prev, s0_ref[0, q], sq_ref[q]) for q in range(nq)]

    def state_step(i):
        c, r0, c0 = inst[i]
        q = i % nq
        res = _dot(jnp.concatenate([qm_ref[i], mq_ref[i]], axis=0),
                   _block_diag(state[q].astype(BF16)))
        y_ref[r0:r0 + CHUNK, c0:c0 + QUAD] = yi_ref[i] + res[:CHUNK]
        state[q] = res[CHUNK:] + nq_ref[i]
        if c == nchunk - 1:
            sq_ref[q] = state[q]
            sout_ref[0, q] = state[q]

    def finish_prev():
        y = y_ref[...]
        mean = _head_sum(y, ones) * (1.0 / HEAD_DIM)
        d = y - mean
        var = _head_sum(d * d, ones) * (1.0 / HEAD_DIM)
        yn = d * lax.rsqrt(var + GN_EPS) * lg_ref[...] + lb_ref[...]
        out_ref[0] = ((yn + bonus_ref[brslot]) * gate_ref[brslot]).astype(out_ref.dtype)

    pending = [functools.partial(state_step, i) for i in range(ni)] + [finish_prev]
    preps = [functools.partial(prep, r0) for r0 in range(0, tt, pb)]
    n_iter = 9 * ni
    sq_lo, sq_hi = 2 * ni, 8 * ni
    emitted = [0]

    def weave():
        emitted[0] += 1
        while pending and (ni + 1 - len(pending)) * n_iter < emitted[0] * (ni + 1):
            pending.pop(0)()
        done = tt // pb - len(preps)
        while preps and emitted[0] > sq_lo and done * (sq_hi - sq_lo) < (emitted[0] - sq_lo) * (tt // pb):
            preps.pop(0)()
            done += 1

    rowc = lax.broadcasted_iota(jnp.int32, (CHUNK, QUAD), 0)
    colc = lax.broadcasted_iota(jnp.int32, (CHUNK, QUAD), 1) % CHUNK
    strict = colc < rowc
    incl = colc <= rowc
    ident = jnp.where(colc == rowc, 1.0, 0.0)

    def tile(ref, i):
        _, r0, c0 = inst[i]
        return ref[rslot, r0:r0 + CHUNK, c0:c0 + QUAD]

    for i in range(ni):
        c, r0, c0 = inst[i]
        pc = pc_ref[rslot, c:c + 1, c0:c0 + QUAD]
        lhs = jnp.concatenate([tile(aa_ref, i), tile(ra_ref, i), (ident * pc).astype(BF16)], axis=0)
        sb = _dot_nt(lhs, _block_diag(tile(bb_ref, i)))
        sk = _dot_nt(lhs, _block_diag(tile(kt_ref, i)))
        a_ab = jnp.where(strict, sb[:CHUNK], 0.0)
        tm_ref[i] = ident + a_ab
        np_ref[i] = a_ab.astype(BF16)
        arb_ref[i] = jnp.where(incl, sb[CHUNK:2 * CHUNK], 0.0).astype(BF16)
        bht_ref[i] = sb[2 * CHUNK:].astype(BF16)
        aak_ref[i, 0:CHUNK] = jnp.where(strict, sk[:CHUNK], 0.0).astype(BF16)
        aak_ref[i, CHUNK:2 * CHUNK] = jnp.where(incl, sk[CHUNK:2 * CHUNK], 0.0).astype(BF16)
        aak_ref[i, 2 * CHUNK:] = sk[2 * CHUNK:].astype(BF16)
        weave()
    for i in range(ni):
        res = _dot(aak_ref[i], _block_diag(tile(v_ref, i)))
        av_ref[i] = res[:CHUNK].astype(BF16)
        rkv_ref[i] = res[CHUNK:]
        weave()
    for i in range(ni):
        n1 = np_ref[i]
        np_ref[i] = _dot(n1, _block_diag(n1)).astype(BF16)
        weave()
    for _ in range(4):
        for i in range(ni):
            npow = np_ref[i]
            tm = tm_ref[i]
            res = _dot(jnp.concatenate([tm.astype(BF16), npow], axis=0), _block_diag(npow))
            tm_ref[i] = tm + res[:CHUNK]
            np_ref[i] = res[CHUNK:].astype(BF16)
            weave()
    for i in range(ni):
        tm = tm_ref[i]
        tmb_ref[i] = (tm + _dot(tm.astype(BF16), _block_diag(np_ref[i]))).astype(BF16)
        weave()
    for i in range(ni):
        rhs = jnp.concatenate([_block_diag(tile(aa_ref, i)), _block_diag(av_ref[i])], axis=1)
        w_ref[i] = _dot(tmb_ref[i], rhs).astype(BF16)
        weave()
    while pending:
        pending.pop(0)()
    while preps:
        preps.pop(0)()
    for i in range(ni):
        c, _, c0 = inst[i]
        w12 = w_ref[i]
        rhs = jnp.concatenate([_block_diag(w12[:, :QUAD]), _block_diag(w12[:, QUAD:])], axis=1)
        res = _dot(jnp.concatenate([arb_ref[i], bht_ref[i]], axis=0), rhs)
        pc = pc_ref[rslot, c:c + 1, c0:c0 + QUAD]
        qm_ref[i] = (tile(ra_ref, i).astype(F32) + res[:CHUNK, :QUAD]).astype(BF16)
        yi_ref[i] = res[:CHUNK, QUAD:] + rkv_ref[i, :CHUNK]
        mq_ref[i] = (ident * pc + res[CHUNK:, :QUAD]).astype(BF16)
        nq_ref[i] = res[CHUNK:, QUAD:] + rkv_ref[i, CHUNK:]


def _rwkv_mix(z, shift_prev, s0q, mu, w0, wa_bf, a0, g2_bf, k_k, k_a, r_k, lnx_g, lnx_b, tt):
    B, T, _ = z.shape
    nq = D_RWKV // QUAD
    ni = (tt // CHUNK) * nq
    idx = jnp.arange(QUAD) // HEAD_DIM
    ones = (idx[:, None] == idx[None, :]).astype(BF16)
    pb = min(tt, 64)
    ti = jnp.arange(pb)
    tri = ((ti[:, None] // CHUNK == ti[None, :] // CHUNK) & (ti[None, :] <= ti[:, None])).astype(BF16)
    nt = T // tt
    ntiles = B * nt
    cur = lambda s: jnp.minimum(s, ntiles - 1)
    prev = lambda s: jnp.maximum(s - 2, 0)
    row = lambda s: (0, 0)
    vec = pl.BlockSpec((1, D_RWKV), row)
    return pl.pallas_call(
        functools.partial(_rwkv_kernel, tt=tt, nt=nt, ntiles=ntiles, pb=pb),
        grid=(ntiles + 2,),
        in_specs=[
            pl.BlockSpec((1, tt, D_SHIFT), lambda s: (cur(s) // nt, cur(s) % nt, 0)),
            pl.BlockSpec((1, 1, D_SHIFT), lambda s: (cur(s) // nt, 0, 0)),
            pl.BlockSpec((1, nq, HEAD_DIM, QUAD), lambda s: (prev(s) // nt, 0, 0, 0)),
            pl.BlockSpec((1, D_SHIFT), row),
            vec,
            pl.BlockSpec((LORA_W + LORA_A, 2 * D_RWKV), row),
            vec,
            pl.BlockSpec((LORA_G, D_RWKV), row),
            vec, vec, vec, vec, vec,
            pl.BlockSpec((QUAD, QUAD), row),
            pl.BlockSpec((pb, pb), row),
        ],
        out_specs=[
            pl.BlockSpec((1, tt, D_RWKV), lambda s: (prev(s) // nt, prev(s) % nt, 0)),
            pl.BlockSpec((1, 1, D_SHIFT), lambda s: (cur(s) // nt, 0, 0)),
            pl.BlockSpec((1, nq, HEAD_DIM, QUAD), lambda s: (prev(s) // nt, 0, 0, 0)),
        ],
        out_shape=[
            jax.ShapeDtypeStruct((B, T, D_RWKV), BF16),
            jax.ShapeDtypeStruct((B, 1, D_SHIFT), F32),
            jax.ShapeDtypeStruct((B, nq, HEAD_DIM, QUAD), F32),
        ],
        scratch_shapes=[
            pltpu.VMEM((1, D_SHIFT), F32),
            pltpu.VMEM((nq, HEAD_DIM, QUAD), F32),
            pltpu.VMEM((2, tt, D_RWKV), BF16),
            pltpu.VMEM((2, tt, D_RWKV), BF16),
            pltpu.VMEM((2, tt, D_RWKV), BF16),
            pltpu.VMEM((2, tt, D_RWKV), BF16),
            pltpu.VMEM((2, tt, D_RWKV), BF16),
            pltpu.VMEM((2, tt // CHUNK, D_RWKV), F32),
            pltpu.VMEM((tt, D_RWKV), F32),
            pltpu.VMEM((ni, CHUNK, QUAD), F32),
            pltpu.VMEM((ni, CHUNK, QUAD), BF16),
            pltpu.VMEM((ni, CHUNK, QUAD), BF16),
            pltpu.VMEM((ni, 3 * CHUNK, QUAD), BF16),
            pltpu.VMEM((ni, CHUNK, QUAD), BF16),
            pltpu.VMEM((ni, CHUNK, QUAD), BF16),
            pltpu.VMEM((ni, CHUNK, QUAD), BF16),
            pltpu.VMEM((ni, 2 * CHUNK, QUAD), F32),
            pltpu.VMEM((ni, CHUNK, 2 * QUAD), BF16),
            pltpu.VMEM((ni, CHUNK, QUAD), BF16),
            pltpu.VMEM((ni, CHUNK, QUAD), BF16),
            pltpu.VMEM((ni, CHUNK, QUAD), F32),
            pltpu.VMEM((ni, CHUNK, QUAD), F32),
            pltpu.VMEM((3, tt, D_RWKV), F32),
            pltpu.VMEM((3, tt, D_RWKV), F32),
        ],
        compiler_params=pltpu.CompilerParams(
            dimension_semantics=("arbitrary",), vmem_limit_bytes=VMEM_LIMIT),
        name="rwkv_mix",
    )(z, shift_prev, s0q, mu, w0, wa_bf, a0, g2_bf, k_k, k_a, r_k, lnx_g, lnx_b, ones, tri)


def _out_ffn_kernel(x_ref, co_ref, ro_ref, mod_ref, gpost_ref, gfpre_ref, gfpost_ref,
                    wo_ref, wg_ref, wu_ref, wd_ref, y_ref):
    m = mod_ref[0]
    tm = x_ref.shape[1]
    nsplit = max(1, tm // 256)
    rows = [slice(i * tm // nsplit, (i + 1) * tm // nsplit) for i in range(nsplit)]

    def mix(r):
        return _dot(co_ref[0, r, :], wo_ref[:D_CONV, :]) + _dot(ro_ref[0, r, :], wo_ref[D_CONV:, :])

    def resid(r, mx):
        return x_ref[0, r, :] + (1.0 + m[2:3]) * _rms(mx, gpost_ref[...])

    def pre(x1):
        return (_rms(x1, gfpre_ref[...]) * (1.0 + m[4:5]) + m[3:4]).astype(BF16)

    def hidden(h):
        gate = _dot(h, wg_ref[...])
        up = _dot(h, wu_ref[...])
        return (gate * _sigmoid(gate) * up).astype(BF16)

    mixes = [mix(r) for r in rows]
    x1 = [None] * nsplit
    act = [None] * nsplit
    for i, r in enumerate(rows):
        x1[i] = resid(r, mixes[i])
        act[i] = hidden(pre(x1[i]))
    for i, r in enumerate(rows):
        f = _dot(act[i], wd_ref[...])
        y_ref[0, r, :] = x1[i] + (1.0 + m[5:6]) * _rms(f, gfpost_ref[...])


def _out_ffn(x, conv_out, rwkv_out, mod, g_post, g_fpre, g_fpost, wo, wg, wu, wd, tm):
    B, T, _ = x.shape
    d_ff = wg.shape[1]
    row = lambda b, t: (0, 0)
    once = dict(pipeline_mode=pl.Buffered(1))
    return pl.pallas_call(
        _out_ffn_kernel,
        grid=(B, T // tm),
        in_specs=[
            pl.BlockSpec((1, tm, D_MODEL), lambda b, t: (b, t, 0)),
            pl.BlockSpec((1, tm, D_CONV), lambda b, t: (b, t, 0)),
            pl.BlockSpec((1, tm, D_RWKV), lambda b, t: (b, t, 0)),
            pl.BlockSpec((1, 6, D_MODEL), lambda b, t: (b, 0, 0)),
            pl.BlockSpec((1, D_MODEL), row),
            pl.BlockSpec((1, D_MODEL), row),
            pl.BlockSpec((1, D_MODEL), row),
            pl.BlockSpec((D_MODEL, D_MODEL), row, **once),
            pl.BlockSpec((D_MODEL, d_ff), row, **once),
            pl.BlockSpec((D_MODEL, d_ff), row, **once),
            pl.BlockSpec((d_ff, D_MODEL), row, **once),
        ],
        out_specs=pl.BlockSpec((1, tm, D_MODEL), lambda b, t: (b, t, 0)),
        out_shape=jax.ShapeDtypeStruct((B, T, D_MODEL), F32),
        compiler_params=pltpu.CompilerParams(
            dimension_semantics=("arbitrary", "arbitrary"), vmem_limit_bytes=VMEM_LIMIT),
        name="out_ffn",
    )(x, conv_out, rwkv_out, mod, g_post, g_fpre, g_fpost, wo, wg, wu, wd)


def _state_to_quads(s):
    B = s.shape[0]
    s = s.reshape(B, N_HEADS // 4, 4, HEAD_DIM, HEAD_DIM)
    return s.transpose(0, 1, 4, 2, 3).reshape(B, N_HEADS // 4, HEAD_DIM, QUAD)


def _quads_to_state(sq):
    B = sq.shape[0]
    s = sq.reshape(B, N_HEADS // 4, HEAD_DIM, 4, HEAD_DIM)
    return s.transpose(0, 1, 3, 4, 2).reshape(B, N_HEADS, HEAD_DIM, HEAD_DIM)


def _layer(x, mod, conv_buf, shift_prev, wkv, p, tiles):
    tin, tr, tffn = tiles
    z, conv_out, new_buf = _inproj_conv(x, mod, p["g_mix_pre"], p["w_in"], conv_buf, p["conv_dw"],
                                        p["conv_b"], p["conv_ln_g"], p["conv_ln_b"], tin)
    rwkv_out, new_shift, sq = _rwkv_mix(
        z, shift_prev[:, None, :], _state_to_quads(wkv), p["mu_shift"], p["w0"], p["wa"], p["a0"],
        p["g2"], p["k_k"], p["k_a"], p["r_k"], p["lnx_g"], p["lnx_b"], tr)
    y = _out_ffn(x, conv_out, rwkv_out, mod, p["g_mix_post"], p["g_ffn_pre"], p["g_ffn_post"],
                 p["w_out"], p["w_gate"], p["w_up"], p["w_down"], tffn)
    return y, new_buf, new_shift[:, 0, :], _quads_to_state(sq)


def _layer_params(l, w):
    row = lambda a: a[l].reshape(1, -1)
    wa = jnp.zeros((LORA_W + LORA_A, 2 * D_RWKV), F32)
    wa = wa.at[:LORA_W, :D_RWKV].set(w["w2"][l]).at[LORA_W:, D_RWKV:].set(w["a2"][l])
    return {
        "g_mix_pre": row(w["g_mix_pre"]), "g_mix_post": row(w["g_mix_post"]),
        "g_ffn_pre": row(w["g_ffn_pre"]), "g_ffn_post": row(w["g_ffn_post"]),
        "w_in": w["w_in"][l].astype(BF16),
        "conv_dw": w["conv_dw"][l], "conv_b": row(w["conv_b"]),
        "conv_ln_g": row(w["conv_ln_g"]), "conv_ln_b": row(w["conv_ln_b"]),
        "mu_shift": row(w["mu_shift"]), "w0": row(w["w0"]), "wa": wa.astype(BF16),
        "a0": row(w["a0"]), "g2": w["g2"][l].astype(BF16),
        "k_k": row(w["k_k"]), "k_a": row(w["k_a"]), "r_k": row(w["r_k"]),
        "lnx_g": row(w["lnx_g"]), "lnx_b": row(w["lnx_b"]),
        "w_out": w["w_out"][l].astype(BF16), "w_gate": w["w_gate"][l].astype(BF16),
        "w_up": w["w_up"][l].astype(BF16), "w_down": w["w_down"][l].astype(BF16),
    }


def _tiles(T):
    tin = min(T, 512)
    tr = min(T, 512)
    tffn = min(T, 1024)
    return tin, tr, tffn


def kernel(x_prompt, x_sample, cache_conv, state_shift, state_wkv, c_prompt, c_sample, w_mod, b_mod, g_mix_pre, g_mix_post, g_ffn_pre, g_ffn_post, w_in, conv_dw, conv_b, conv_ln_g, conv_ln_b, mu_shift, w0, w2, a0, a2, g2, k_k, k_a, r_k, lnx_g, lnx_b, w_out, w_gate, w_up, w_down):
    w = dict(g_mix_pre=g_mix_pre, g_mix_post=g_mix_post, g_ffn_pre=g_ffn_pre, g_ffn_post=g_ffn_post,
             w_in=w_in, conv_dw=conv_dw, conv_b=conv_b, conv_ln_g=conv_ln_g, conv_ln_b=conv_ln_b,
             mu_shift=mu_shift, w0=w0, w2=w2, a0=a0, a2=a2, g2=g2, k_k=k_k, k_a=k_a, r_k=r_k,
             lnx_g=lnx_g, lnx_b=lnx_b, w_out=w_out, w_gate=w_gate, w_up=w_up, w_down=w_down)
    depth = w_mod.shape[0]
    bp, tp, _ = x_prompt.shape
    bs, ts, _ = x_sample.shape
    mod = _modulation(jnp.concatenate([c_prompt, c_sample], axis=0), w_mod, b_mod)
    mod = mod.reshape(depth, bp + bs, 6, D_MODEL)
    zero_conv = jnp.zeros((bp, CONV_BUF, D_CONV), F32)
    zero_shift = jnp.zeros((bp, D_SHIFT), F32)
    zero_wkv = jnp.zeros((bp, N_HEADS, HEAD_DIM, HEAD_DIM), F32)
    yp, ys = x_prompt, x_sample
    cp, sp, wp, cs, ss, wsm = [], [], [], [], [], []
    for l in range(depth):
        p = _layer_params(l, w)
        yp, b1, s1, k1 = _layer(yp, mod[l, :bp], zero_conv, zero_shift, zero_wkv, p, _tiles(tp))
        ys, b2, s2, k2 = _layer(ys, mod[l, bp:], cache_conv[l], state_shift[l], state_wkv[l], p,
                                _tiles(ts))
        cp.append(b1); sp.append(s1); wp.append(k1)
        cs.append(b2); ss.append(s2); wsm.append(k2)
    return (yp, ys, jnp.stack(cp), jnp.stack(sp), jnp.stack(wp),
            jnp.stack(cs), jnp.stack(ss), jnp.stack(wsm))
```

```python
import functools
import math

import jax
import jax.numpy as jnp
from jax import lax
from jax.experimental import pallas as pl
from jax.experimental.pallas import tpu as pltpu

F32 = jnp.float32
BF16 = jnp.bfloat16

D_MODEL = 1024
D_CONV = 512
D_RWKV = 512
HEAD_DIM = 64
N_HEADS = 8
CONV_WIDTH = 31
CONV_BUF = 30
LORA_W = 64
LORA_A = 64
LORA_G = 128
D_SHIFT = 3 * D_RWKV + LORA_W + LORA_A + LORA_G
D_IN = 2 * D_CONV + D_SHIFT
RMS_EPS = 1e-6
LN_EPS = 1e-5
GN_EPS = 64e-5
L2_EPS = 1e-12

CHUNK = 64
QUAD = 4 * HEAD_DIM
CONV_HALO = 32
VMEM_LIMIT = 56 * 1024 * 1024


def _dot(a, b):
    return jnp.dot(a, b, preferred_element_type=F32)


def _dot_nt(a, b):
    return lax.dot_general(a, b, (((1,), (1,)), ((), ())), preferred_element_type=F32)


def _dot_hilo_lhs(m, x):
    hi = x.astype(BF16)
    lo = (x - hi.astype(F32)).astype(BF16)
    return _dot(m, hi) + _dot(m, lo)


def _head_sum(x, ones):
    xb = x.astype(BF16)
    return jnp.concatenate(
        [_dot(xb[:, q * QUAD:(q + 1) * QUAD], ones) for q in range(D_RWKV // QUAD)], axis=1)


def _sigmoid(x):
    return 1.0 / (1.0 + jnp.exp(-x))


def _rms(x, g):
    return x * lax.rsqrt(jnp.mean(x * x, axis=-1, keepdims=True) + RMS_EPS) * g


def _mod_kernel(c_ref, w_ref, b_ref, o_ref):
    c = c_ref[...]
    s = c * _sigmoid(c)
    o_ref[0] = jnp.dot(s, w_ref[0], preferred_element_type=F32,
                       precision=lax.Precision.HIGHEST) + b_ref[0]


def _modulation(c_all, w_mod, b_mod):
    L = w_mod.shape[0]
    bc = c_all.shape[0]
    n = w_mod.shape[2]
    tn = 1024
    return pl.pallas_call(
        _mod_kernel,
        grid=(L, n // tn),
        in_specs=[
            pl.BlockSpec((bc, D_MODEL), lambda l, j: (0, 0)),
            pl.BlockSpec((1, D_MODEL, tn), lambda l, j: (l, 0, j)),
            pl.BlockSpec((1, 1, tn), lambda l, j: (l, 0, j)),
        ],
        out_specs=pl.BlockSpec((1, bc, tn), lambda l, j: (l, 0, j)),
        out_shape=jax.ShapeDtypeStruct((L, bc, n), F32),
        compiler_params=pltpu.CompilerParams(
            dimension_semantics=("arbitrary", "arbitrary"), vmem_limit_bytes=VMEM_LIMIT),
        name="modulation",
    )(c_all, w_mod, b_mod.reshape(L, 1, n))


def _inproj_conv_kernel(x_ref, mod_ref, g_ref, w_ref, buf_ref, dw_ref, cb_ref, lg_ref, lb_ref,
                        z_ref, out_ref, nbuf_ref, win_ref, *, tt, rb):
    first = pl.program_id(1) == 0
    pad = CONV_HALO - CONV_BUF
    strip = 128

    m = mod_ref[0]
    hb = (_rms(x_ref[0], g_ref[...]) * (1.0 + m[1:2]) + m[0:1]).astype(BF16)
    u = _dot(hb, w_ref[:, :2 * D_CONV])
    win_ref[0:CONV_HALO, :] = jnp.where(first, buf_ref[0], win_ref[0:CONV_HALO, :])
    win_ref[CONV_HALO:CONV_HALO + tt, :] = u[:, :D_CONV] * _sigmoid(u[:, D_CONV:])
    z_ref[0] = _dot(hb, w_ref[:, 2 * D_CONV:]).astype(z_ref.dtype)

    for r in range(tt // rb):
        base = r * rb
        accs = []
        for c0 in range(0, D_CONV, strip):
            cols = slice(c0, c0 + strip)
            acc = jnp.broadcast_to(cb_ref[:, cols], (rb, strip))
            for s in range(8):
                part = None
                for a in range((pad + CONV_WIDTH + 7) // 8):
                    j = 8 * a + s
                    if j < pad or j >= pad + CONV_WIDTH:
                        continue
                    lo = base + 8 * a
                    term = win_ref[lo:lo + rb + (8 if s else 0), cols] * dw_ref[j - pad:j - pad + 1, cols]
                    part = term if part is None else part + term
                acc = acc + part[s:s + rb]
            accs.append(acc)
        acc = jnp.concatenate(accs, axis=1)
        mu = jnp.mean(acc, axis=-1, keepdims=True)
        d = acc - mu
        var = jnp.mean(d * d, axis=-1, keepdims=True)
        y = d * lax.rsqrt(var + LN_EPS) * lg_ref[...] + lb_ref[...]
        out_ref[0, base:base + rb, :] = (y * _sigmoid(y)).astype(out_ref.dtype)

    tail = win_ref[tt:tt + CONV_HALO, :]
    win_ref[0:CONV_HALO, :] = tail
    nbuf_ref[0] = tail


def _inproj_conv(x, mod, g, w_in_bf, buf, dw, cb, lg, lb, tt):
    B, T, _ = x.shape
    rb = min(tt, 128)
    pad = CONV_HALO - CONV_BUF
    buf = jnp.pad(buf, ((0, 0), (pad, 0), (0, 0)))
    row = lambda b, t: (0, 0)
    z, out, nbuf = pl.pallas_call(
        functools.partial(_inproj_conv_kernel, tt=tt, rb=rb),
        grid=(B, T // tt),
        in_specs=[
            pl.BlockSpec((1, tt, D_MODEL), lambda b, t: (b, t, 0)),
            pl.BlockSpec((1, 6, D_MODEL), lambda b, t: (b, 0, 0)),
            pl.BlockSpec((1, D_MODEL), row),
            pl.BlockSpec((D_MODEL, D_IN), row),
            pl.BlockSpec((1, CONV_HALO, D_CONV), lambda b, t: (b, 0, 0)),
            pl.BlockSpec((CONV_WIDTH, D_CONV), row),
            pl.BlockSpec((1, D_CONV), row),
            pl.BlockSpec((1, D_CONV), row),
            pl.BlockSpec((1, D_CONV), row),
        ],
        out_specs=[
            pl.BlockSpec((1, tt, D_SHIFT), lambda b, t: (b, t, 0)),
            pl.BlockSpec((1, tt, D_CONV), lambda b, t: (b, t, 0)),
            pl.BlockSpec((1, CONV_HALO, D_CONV), lambda b, t: (b, 0, 0)),
        ],
        out_shape=[
            jax.ShapeDtypeStruct((B, T, D_SHIFT), BF16),
            jax.ShapeDtypeStruct((B, T, D_CONV), BF16),
            jax.ShapeDtypeStruct((B, CONV_HALO, D_CONV), F32),
        ],
        scratch_shapes=[pltpu.VMEM((CONV_HALO + tt, D_CONV), F32)],
        compiler_params=pltpu.CompilerParams(
            dimension_semantics=("arbitrary", "arbitrary"), vmem_limit_bytes=VMEM_LIMIT),
        name="inproj_conv",
    )(x, mod, g, w_in_bf, buf, dw, cb, lg, lb)
    return z, out, nbuf[:, pad:, :]


def _block_diag(x):
    half = QUAD // 2
    lo = lax.broadcasted_iota(jnp.int32, (HEAD_DIM, half), 1) < HEAD_DIM
    zero = jnp.zeros((HEAD_DIM, half), x.dtype)
    xl, xr = x[:, :half], x[:, half:]
    rows = [
        jnp.concatenate([jnp.where(lo, xl, zero), zero], axis=1),
        jnp.concatenate([jnp.where(lo, zero, xl), zero], axis=1),
        jnp.concatenate([zero, jnp.where(lo, xr, zero)], axis=1),
        jnp.concatenate([zero, jnp.where(lo, zero, xr)], axis=1),
    ]
    return jnp.concatenate(rows, axis=0)


def _rwkv_kernel(z_ref, sprev_ref, s0_ref, mu_ref, w0_ref, wa_ref, a0_ref, g2_ref, kk_ref, ka_ref,
                 rk_ref, lg_ref, lb_ref, ones_ref, tri_ref,
                 out_ref, nshift_ref, sout_ref,
                 carry_ref, sq_ref, ra_ref, aa_ref, bb_ref, kt_ref, v_ref,
                 pc_ref, y_ref, tm_ref, np_ref, arb_ref, aak_ref, bht_ref, tmb_ref, av_ref,
                 rkv_ref, w_ref, qm_ref, mq_ref, nq_ref, yi_ref, bonus_ref, gate_ref,
                 *, tt, nt, ntiles, pb):
    step = pl.program_id(0)
    first = jnp.minimum(step, ntiles - 1) % nt == 0
    first_prev = jnp.maximum(step - 2, 0) % nt == 0
    wslot = step % 2
    rslot = 1 - wslot
    bwslot = step % 3
    brslot = (step + 1) % 3
    nq = D_RWKV // QUAD
    nchunk = tt // CHUNK
    inst = [(c, c * CHUNK, q * QUAD) for c in range(nchunk) for q in range(nq)]
    ni = len(inst)
    ones = ones_ref[...]

    def prep(r0):
        z = z_ref[0, r0:r0 + pb, :].astype(F32)
        if r0 == 0:
            last_row = jnp.where(first, sprev_ref[0], carry_ref[...])
        else:
            last_row = z_ref[0, r0 - 16:r0, :].astype(F32)[15:16, :]
        rows = lax.broadcasted_iota(jnp.int32, (pb, 1), 0)
        z_prev = jnp.where(rows == 0, last_row, pltpu.roll(z, 1, 0))
        if r0 + pb == tt:
            carry_ref[...] = z[pb - 1:pb, :]
            nshift_ref[0] = z[pb - 1:pb, :]
        zm = z + (z_prev - z) * mu_ref[...]
        r = zm[:, 0:D_RWKV]
        k = zm[:, D_RWKV:2 * D_RWKV]
        v = zm[:, 2 * D_RWKV:3 * D_RWKV]
        zwa = zm[:, 3 * D_RWKV:3 * D_RWKV + LORA_W + LORA_A]
        zg = zm[:, 3 * D_RWKV + LORA_W + LORA_A:]
        lane = lax.broadcasted_iota(jnp.int32, zwa.shape, 1)
        zwa = jnp.where(lane < LORA_W, jnp.tanh(zwa), zwa)
        wa = _dot(zwa.astype(BF16), wa_ref[...])
        lw = -math.exp(-0.5) * _sigmoid(w0_ref[...] + wa[:, :D_RWKV])
        a = _sigmoid(a0_ref[...] + wa[:, D_RWKV:])
        g = _dot(_sigmoid(zg).astype(BF16), g2_ref[...])
        kk = k * kk_ref[...]
        k = k * (1.0 + (a - 1.0) * ka_ref[...])
        sums = _head_sum(jnp.concatenate([kk * kk, r * k * rk_ref[...]], axis=0), ones)
        kk = kk * lax.rsqrt(sums[:pb] + L2_EPS)
        b = kk * a
        bonus = sums[pb:] * v
        cum = _dot_hilo_lhs(tri_ref[...], lw)
        e_neg = jnp.exp(-cum)
        rs = slice(r0, r0 + pb)
        ra_ref[wslot, rs, :] = (r * jnp.exp(cum)).astype(BF16)
        aa_ref[wslot, rs, :] = (-kk * jnp.exp(cum - lw)).astype(BF16)
        bb_ref[wslot, rs, :] = (b * e_neg).astype(BF16)
        kt_ref[wslot, rs, :] = (k * e_neg).astype(BF16)
        v_ref[wslot, rs, :] = v.astype(BF16)
        for c in range(pb // CHUNK):
            cc = r0 // CHUNK + c
            pc_ref[wslot, cc:cc + 1, :] = jnp.exp(cum[(c + 1) * CHUNK - 1:(c + 1) * CHUNK, :])
        bonus_ref[bwslot, rs, :] = bonus
        gate_ref[bwslot, rs, :] = g

    state = [jnp.where(first_prev, s0_ref[0, q], sq_ref[q]) for q in range(nq)]

    def state_step(i):
        c, r0, c0 = inst[i]
        q = i % nq
        res = _dot(jnp.concatenate([qm_ref[i], mq_ref[i]], axis=0),
                   _block_diag(state[q].astype(BF16)))
        y_ref[r0:r0 + CHUNK, c0:c0 + QUAD] = yi_ref[i] + res[:CHUNK]
        state[q] = res[CHUNK:] + nq_ref[i]
        if c == nchunk - 1:
            sq_ref[q] = state[q]
            sout_ref[0, q] = state[q]

    def finish_prev():
        y = y_ref[...]
        mean = _head_sum(y, ones) * (1.0 / HEAD_DIM)
        d = y - mean
        var = _head_sum(d * d, ones) * (1.0 / HEAD_DIM)
        yn = d * lax.rsqrt(var + GN_EPS) * lg_ref[...] + lb_ref[...]
        out_ref[0] = ((yn + bonus_ref[brslot]) * gate_ref[brslot]).astype(out_ref.dtype)

    pending = [functools.partial(state_step, i) for i in range(ni)] + [finish_prev]
    preps = [functools.partial(prep, r0) for r0 in range(0, tt, pb)]
    n_iter = 9 * ni
    sq_lo, sq_hi = 2 * ni, 8 * ni
    emitted = [0]

    def weave():
        emitted[0] += 1
        while pending and (ni + 1 - len(pending)) * n_iter < emitted[0] * (ni + 1):
            pending.pop(0)()
        done = tt // pb - len(preps)
        while preps and emitted[0] > sq_lo and done * (sq_hi - sq_lo) < (emitted[0] - sq_lo) * (tt // pb):
            preps.pop(0)()
            done += 1

    rowc = lax.broadcasted_iota(jnp.int32, (CHUNK, QUAD), 0)
    colc = lax.broadcasted_iota(jnp.int32, (CHUNK, QUAD), 1) % CHUNK
    strict = colc < rowc
    incl = colc <= rowc
    ident = jnp.where(colc == rowc, 1.0, 0.0)

    def tile(ref, i):
        _, r0, c0 = inst[i]
        return ref[rslot, r0:r0 + CHUNK, c0:c0 + QUAD]

    for i in range(ni):
        c, r0, c0 = inst[i]
        pc = pc_ref[rslot, c:c + 1, c0:c0 + QUAD]
        lhs = jnp.concatenate([tile(aa_ref, i), tile(ra_ref, i), (ident * pc).astype(BF16)], axis=0)
        sb = _dot_nt(lhs, _block_diag(tile(bb_ref, i)))
        sk = _dot_nt(lhs, _block_diag(tile(kt_ref, i)))
        a_ab = jnp.where(strict, sb[:CHUNK], 0.0)
        tm_ref[i] = ident + a_ab
        np_ref[i] = a_ab.astype(BF16)
        arb_ref[i] = jnp.where(incl, sb[CHUNK:2 * CHUNK], 0.0).astype(BF16)
        bht_ref[i] = sb[2 * CHUNK:].astype(BF16)
        aak_ref[i, 0:CHUNK] = jnp.where(strict, sk[:CHUNK], 0.0).astype(BF16)
        aak_ref[i, CHUNK:2 * CHUNK] = jnp.where(incl, sk[CHUNK:2 * CHUNK], 0.0).astype(BF16)
        aak_ref[i, 2 * CHUNK:] = sk[2 * CHUNK:].astype(BF16)
        weave()
    for i in range(ni):
        res = _dot(aak_ref[i], _block_diag(tile(v_ref, i)))
        av_ref[i] = res[:CHUNK].astype(BF16)
        rkv_ref[i] = res[CHUNK:]
        weave()
    for i in range(ni):
        n1 = np_ref[i]
        np_ref[i] = _dot(n1, _block_diag(n1)).astype(BF16)
        weave()
    for _ in range(4):
        for i in range(ni):
            npow = np_ref[i]
            tm = tm_ref[i]
            res = _dot(jnp.concatenate([tm.astype(BF16), npow], axis=0), _block_diag(npow))
            tm_ref[i] = tm + res[:CHUNK]
            np_ref[i] = res[CHUNK:].astype(BF16)
            weave()
    for i in range(ni):
        tm = tm_ref[i]
        tmb_ref[i] = (tm + _dot(tm.astype(BF16), _block_diag(np_ref[i]))).astype(BF16)
        weave()
    for i in range(ni):
        rhs = jnp.concatenate([_block_diag(tile(aa_ref, i)), _block_diag(av_ref[i])], axis=1)
        w_ref[i] = _dot(tmb_ref[i], rhs).astype(BF16)
        weave()
    while pending:
        pending.pop(0)()
    while preps:
        preps.pop(0)()
    for i in range(ni):
        c, _, c0 = inst[i]
        w12 = w_ref[i]
        rhs = jnp.concatenate([_block_diag(w12[:, :QUAD]), _block_diag(w12[:, QUAD:])], axis=1)
        res = _dot(jnp.concatenate([arb_ref[i], bht_ref[i]], axis=0), rhs)
        pc = pc_ref[rslot, c:c + 1, c0:c0 + QUAD]
        qm_ref[i] = (tile(ra_ref, i).astype(F32) + res[:CHUNK, :QUAD]).astype(BF16)
        yi_ref[i] = res[:CHUNK, QUAD:] + rkv_ref[i, :CHUNK]
        mq_ref[i] = (ident * pc + res[CHUNK:, :QUAD]).astype(BF16)
        nq_ref[i] = res[CHUNK:, QUAD:] + rkv_ref[i, CHUNK:]


def _rwkv_mix(z, shift_prev, s0q, mu, w0, wa_bf, a0, g2_bf, k_k, k_a, r_k, lnx_g, lnx_b, tt):
    B, T, _ = z.shape
    nq = D_RWKV // QUAD
    ni = (tt // CHUNK) * nq
    idx = jnp.arange(QUAD) // HEAD_DIM
    ones = (idx[:, None] == idx[None, :]).astype(BF16)
    pb = min(tt, 64)
    ti = jnp.arange(pb)
    tri = ((ti[:, None] // CHUNK == ti[None, :] // CHUNK) & (ti[None, :] <= ti[:, None])).astype(BF16)
    nt = T // tt
    ntiles = B * nt
    cur = lambda s: jnp.minimum(s, ntiles - 1)
    prev = lambda s: jnp.maximum(s - 2, 0)
    row = lambda s: (0, 0)
    vec = pl.BlockSpec((1, D_RWKV), row)
    return pl.pallas_call(
        functools.partial(_rwkv_kernel, tt=tt, nt=nt, ntiles=ntiles, pb=pb),
        grid=(ntiles + 2,),
        in_specs=[
            pl.BlockSpec((1, tt, D_SHIFT), lambda s: (cur(s) // nt, cur(s) % nt, 0)),
            pl.BlockSpec((1, 1, D_SHIFT), lambda s: (cur(s) // nt, 0, 0)),
            pl.BlockSpec((1, nq, HEAD_DIM, QUAD), lambda s: (prev(s) // nt, 0, 0, 0)),
            pl.BlockSpec((1, D_SHIFT), row),
            vec,
            pl.BlockSpec((LORA_W + LORA_A, 2 * D_RWKV), row),
            vec,
            pl.BlockSpec((LORA_G, D_RWKV), row),
            vec, vec, vec, vec, vec,
            pl.BlockSpec((QUAD, QUAD), row),
            pl.BlockSpec((pb, pb), row),
        ],
        out_specs=[
            pl.BlockSpec((1, tt, D_RWKV), lambda s: (prev(s) // nt, prev(s) % nt, 0)),
            pl.BlockSpec((1, 1, D_SHIFT), lambda s: (cur(s) // nt, 0, 0)),
            pl.BlockSpec((1, nq, HEAD_DIM, QUAD), lambda s: (prev(s) // nt, 0, 0, 0)),
        ],
        out_shape=[
            jax.ShapeDtypeStruct((B, T, D_RWKV), BF16),
            jax.ShapeDtypeStruct((B, 1, D_SHIFT), F32),
            jax.ShapeDtypeStruct((B, nq, HEAD_DIM, QUAD), F32),
        ],
        scratch_shapes=[
            pltpu.VMEM((1, D_SHIFT), F32),
            pltpu.VMEM((nq, HEAD_DIM, QUAD), F32),
            pltpu.VMEM((2, tt, D_RWKV), BF16),
            pltpu.VMEM((2, tt, D_RWKV), BF16),
            pltpu.VMEM((2, tt, D_RWKV), BF16),
            pltpu.VMEM((2, tt, D_RWKV), BF16),
            pltpu.VMEM((2, tt, D_RWKV), BF16),
            pltpu.VMEM((2, tt // CHUNK, D_RWKV), F32),
            pltpu.VMEM((tt, D_RWKV), F32),
            pltpu.VMEM((ni, CHUNK, QUAD), F32),
            pltpu.VMEM((ni, CHUNK, QUAD), BF16),
            pltpu.VMEM((ni, CHUNK, QUAD), BF16),
            pltpu.VMEM((ni, 3 * CHUNK, QUAD), BF16),
            pltpu.VMEM((ni, CHUNK, QUAD), BF16),
            pltpu.VMEM((ni, CHUNK, QUAD), BF16),
            pltpu.VMEM((ni, CHUNK, QUAD), BF16),
            pltpu.VMEM((ni, 2 * CHUNK, QUAD), F32),
            pltpu.VMEM((ni, CHUNK, 2 * QUAD), BF16),
            pltpu.VMEM((ni, CHUNK, QUAD), BF16),
            pltpu.VMEM((ni, CHUNK, QUAD), BF16),
            pltpu.VMEM((ni, CHUNK, QUAD), F32),
            pltpu.VMEM((ni, CHUNK, QUAD), F32),
            pltpu.VMEM((3, tt, D_RWKV), F32),
            pltpu.VMEM((3, tt, D_RWKV), F32),
        ],
        compiler_params=pltpu.CompilerParams(
            dimension_semantics=("arbitrary",), vmem_limit_bytes=VMEM_LIMIT),
        name="rwkv_mix",
    )(z, shift_prev, s0q, mu, w0, wa_bf, a0, g2_bf, k_k, k_a, r_k, lnx_g, lnx_b, ones, tri)


def _out_ffn_kernel(x_ref, co_ref, ro_ref, mod_ref, gpost_ref, gfpre_ref, gfpost_ref,
                    wo_ref, wg_ref, wu_ref, wd_ref, y_ref):
    m = mod_ref[0]
    tm = x_ref.shape[1]
    nsplit = max(1, tm // 256)
    rows = [slice(i * tm // nsplit, (i + 1) * tm // nsplit) for i in range(nsplit)]

    def mix(r):
        return _dot(co_ref[0, r, :], wo_ref[:D_CONV, :]) + _dot(ro_ref[0, r, :], wo_ref[D_CONV:, :])

    def resid(r, mx):
        return x_ref[0, r, :] + (1.0 + m[2:3]) * _rms(mx, gpost_ref[...])

    def pre(x1):
        return (_rms(x1, gfpre_ref[...]) * (1.0 + m[4:5]) + m[3:4]).astype(BF16)

    def hidden(h):
        gate = _dot(h, wg_ref[...])
        up = _dot(h, wu_ref[...])
        return (gate * _sigmoid(gate) * up).astype(BF16)

    mixes = [mix(r) for r in rows]
    x1 = [None] * nsplit
    act = [None] * nsplit
    for i, r in enumerate(rows):
        x1[i] = resid(r, mixes[i])
        act[i] = hidden(pre(x1[i]))
    for i, r in enumerate(rows):
        f = _dot(act[i], wd_ref[...])
        y_ref[0, r, :] = x1[i] + (1.0 + m[5:6]) * _rms(f, gfpost_ref[...])


def _out_ffn(x, conv_out, rwkv_out, mod, g_post, g_fpre, g_fpost, wo, wg, wu, wd, tm):
    B, T, _ = x.shape
    d_ff = wg.shape[1]
    row = lambda b, t: (0, 0)
    once = dict(pipeline_mode=pl.Buffered(1))
    return pl.pallas_call(
        _out_ffn_kernel,
        grid=(B, T // tm),
        in_specs=[
            pl.BlockSpec((1, tm, D_MODEL), lambda b, t: (b, t, 0)),
            pl.BlockSpec((1, tm, D_CONV), lambda b, t: (b, t, 0)),
            pl.BlockSpec((1, tm, D_RWKV), lambda b, t: (b, t, 0)),
            pl.BlockSpec((1, 6, D_MODEL), lambda b, t: (b, 0, 0)),
            pl.BlockSpec((1, D_MODEL), row),
            pl.BlockSpec((1, D_MODEL), row),
            pl.BlockSpec((1, D_MODEL), row),
            pl.BlockSpec((D_MODEL, D_MODEL), row, **once),
            pl.BlockSpec((D_MODEL, d_ff), row, **once),
            pl.BlockSpec((D_MODEL, d_ff), row, **once),
            pl.BlockSpec((d_ff, D_MODEL), row, **once),
        ],
        out_specs=pl.BlockSpec((1, tm, D_MODEL), lambda b, t: (b, t, 0)),
        out_shape=jax.ShapeDtypeStruct((B, T, D_MODEL), F32),
        compiler_params=pltpu.CompilerParams(
            dimension_semantics=("arbitrary", "arbitrary"), vmem_limit_bytes=VMEM_LIMIT),
        name="out_ffn",
    )(x, conv_out, rwkv_out, mod, g_post, g_fpre, g_fpost, wo, wg, wu, wd)


def _state_to_quads(s):
    B = s.shape[0]
    s = s.reshape(B, N_HEADS // 4, 4, HEAD_DIM, HEAD_DIM)
    return s.transpose(0, 1, 4, 2, 3).reshape(B, N_HEADS // 4, HEAD_DIM, QUAD)


def _quads_to_state(sq):
    B = sq.shape[0]
    s = sq.reshape(B, N_HEADS // 4, HEAD_DIM, 4, HEAD_DIM)
    return s.transpose(0, 1, 3, 4, 2).reshape(B, N_HEADS, HEAD_DIM, HEAD_DIM)


def _layer(x, mod, conv_buf, shift_prev, wkv, p, tiles):
    tin, tr, tffn = tiles
    z, conv_out, new_buf = _inproj_conv(x, mod, p["g_mix_pre"], p["w_in"], conv_buf, p["conv_dw"],
                                        p["conv_b"], p["conv_ln_g"], p["conv_ln_b"], tin)
    rwkv_out, new_shift, sq = _rwkv_mix(
        z, shift_prev[:, None, :], _state_to_quads(wkv), p["mu_shift"], p["w0"], p["wa"], p["a0"],
        p["g2"], p["k_k"], p["k_a"], p["r_k"], p["lnx_g"], p["lnx_b"], tr)
    y = _out_ffn(x, conv_out, rwkv_out, mod, p["g_mix_post"], p["g_ffn_pre"], p["g_ffn_post"],
                 p["w_out"], p["w_gate"], p["w_up"], p["w_down"], tffn)
    return y, new_buf, new_shift[:, 0, :], _quads_to_state(sq)


def _layer_params(l, w):
    row = lambda a: a[l].reshape(1, -1)
    wa = jnp.zeros((LORA_W + LORA_A, 2 * D_RWKV), F32)
    wa = wa.at[:LORA_W, :D_RWKV].set(w["w2"][l]).at[LORA_W:, D_RWKV:].set(w["a2"][l])
    return {
        "g_mix_pre": row(w["g_mix_pre"]), "g_mix_post": row(w["g_mix_post"]),
        "g_ffn_pre": row(w["g_ffn_pre"]), "g_ffn_post": row(w["g_ffn_post"]),
        "w_in": w["w_in"][l].astype(BF16),
        "conv_dw": w["conv_dw"][l], "conv_b": row(w["conv_b"]),
        "conv_ln_g": row(w["conv_ln_g"]), "conv_ln_b": row(w["conv_ln_b"]),
        "mu_shift": row(w["mu_shift"]), "w0": row(w["w0"]), "wa": wa.astype(BF16),
        "a0": row(w["a0"]), "g2": w["g2"][l].astype(BF16),
        "k_k": row(w["k_k"]), "k_a": row(w["k_a"]), "r_k": row(w["r_k"]),
        "lnx_g": row(w["lnx_g"]), "lnx_b": row(w["lnx_b"]),
        "w_out": w["w_out"][l].astype(BF16), "w_gate": w["w_gate"][l].astype(BF16),
        "w_up": w["w_up"][l].astype(BF16), "w_down": w["w_down"][l].astype(BF16),
    }


def _tiles(T):
    tin = min(T, 512)
    tr = min(T, 512)
    tffn = min(T, 1024)
    return tin, tr, tffn


def kernel(x_prompt, x_sample, cache_conv, state_shift, state_wkv, c_prompt, c_sample, w_mod, b_mod, g_mix_pre, g_mix_post, g_ffn_pre, g_ffn_post, w_in, conv_dw, conv_b, conv_ln_g, conv_ln_b, mu_shift, w0, w2, a0, a2, g2, k_k, k_a, r_k, lnx_g, lnx_b, w_out, w_gate, w_up, w_down):
    w = dict(g_mix_pre=g_mix_pre, g_mix_post=g_mix_post, g_ffn_pre=g_ffn_pre, g_ffn_post=g_ffn_post,
             w_in=w_in, conv_dw=conv_dw, conv_b=conv_b, conv_ln_g=conv_ln_g, conv_ln_b=conv_ln_b,
             mu_shift=mu_shift, w0=w0, w2=w2, a0=a0, a2=a2, g2=g2, k_k=k_k, k_a=k_a, r_k=r_k,
             lnx_g=lnx_g, lnx_b=lnx_b, w_out=w_out, w_gate=w_gate, w_up=w_up, w_down=w_down)
    depth = w_mod.shape[0]
    bp, tp, _ = x_prompt.shape
    bs, ts, _ = x_sample.shape
    mod = _modulation(jnp.concatenate([c_prompt, c_sample], axis=0), w_mod, b_mod)
    mod = mod.reshape(depth, bp + bs, 6, D_MODEL)
    zero_conv = jnp.zeros((bp, CONV_BUF, D_CONV), F32)
    zero_shift = jnp.zeros((bp, D_SHIFT), F32)
    zero_wkv = jnp.zeros((bp, N_HEADS, HEAD_DIM, HEAD_DIM), F32)
    yp, ys = x_prompt, x_sample
    cp, sp, wp, cs, ss, wsm = [], [], [], [], [], []
    for l in range(depth):
        p = _layer_params(l, w)
        yp, b1, s1, k1 = _layer(yp, mod[l, :bp], zero_conv, zero_shift, zero_wkv, p, _tiles(tp))
        ys, b2, s2, k2 = _layer(ys, mod[l, bp:], cache_conv[l], state_shift[l], state_wkv[l], p,
                                _tiles(ts))
        cp.append(b1); sp.append(s1); wp.append(k1)
        cs.append(b2); ss.append(s2); wsm.append(k2)
    return (yp, ys, jnp.stack(cp), jnp.stack(sp), jnp.stack(wp),
            jnp.stack(cs), jnp.stack(ss), jnp.stack(wsm))
```
